```python
import jax, jax.numpy as jnp
from jax import lax
import numpy as np

D_MODEL = 1024
BATCH = 8
SEQ = 8192
DEPTH = 4

GRID_W = 64
CTX_LEN = 256
N_HEADS = 16
HEAD_DIM = D_MODEL // N_HEADS
NA_ROWS = 8
NA_COLS = 16
CONV_WIDTH = 3
FFN_HIDDEN = -(-8 * D_MODEL // (3 * 256)) * 256
EPS = 1e-6
NEG_INF = -1e30

kernel_name = "hybrid_conv_natten_prefix_dit"


def _rms_norm(x, g):
    xf = x.astype(jnp.float32)
    y = xf * lax.rsqrt(jnp.mean(xf * xf, axis=-1, keepdims=True) + EPS)
    return (y * g.astype(jnp.float32)).astype(x.dtype)


def _modulate(h, shift, scale):
    return h * (1 + scale) + shift


def _swiglu(h, w_in, w_out):
    gate, up = jnp.split(h @ w_in, 2, axis=-1)
    return (jax.nn.silu(gate) * up) @ w_out


def _short_conv_mixer(h, w_in, conv_w, w_out):
    b_gate, c_gate, u = jnp.split(h @ w_in, 3, axis=-1)
    z = c_gate * u
    n = z.shape[1]
    pad = CONV_WIDTH // 2
    zp = jnp.pad(z, ((0, 0), (pad, pad), (0, 0)))
    zc = sum(conv_w[j] * zp[:, j:j + n] for j in range(CONV_WIDTH))
    return (b_gate * zc) @ w_out


def _qkv_heads(h, w_qkv, g_q, g_k):
    q, k, v = jnp.split(h @ w_qkv, 3, axis=-1)
    shp = h.shape[:-1] + (N_HEADS, HEAD_DIM)
    return _rms_norm(q.reshape(shp), g_q), _rms_norm(k.reshape(shp), g_k), v.reshape(shp)


def _neighbourhood_attention(h, hc, w_qkv, g_q, g_k, rpb, w_out, with_ctx_out):
    bsz, n, _ = h.shape
    rows = n // GRID_W
    kh = min(NA_ROWS, rows)
    scale = HEAD_DIM ** -0.5
    q, k, v = _qkv_heads(h, w_qkv, g_q, g_k)
    qc, kc, vc = _qkv_heads(hc, w_qkv, g_q, g_k)
    q_g = q.reshape(bsz, rows, GRID_W, N_HEADS, HEAD_DIM)
    k_g = k.reshape(bsz, rows, GRID_W, N_HEADS, HEAD_DIM)
    v_g = v.reshape(bsz, rows, GRID_W, N_HEADS, HEAD_DIM)

    col = jnp.arange(GRID_W)
    col_start = jnp.clip(col - NA_COLS // 2, 0, GRID_W - NA_COLS)
    col_ok = (col[None, :] >= col_start[:, None]) & (col[None, :] < col_start[:, None] + NA_COLS)
    dc = jnp.clip(col[None, :] - col[:, None] + NA_COLS - 1, 0, 2 * NA_COLS - 2)

    def row_block(r):
        r_start = jnp.clip(r - kh // 2, 0, rows - kh)
        q_r = lax.dynamic_index_in_dim(q_g, r, axis=1, keepdims=False)
        k_r = lax.dynamic_slice_in_dim(k_g, r_start, kh, axis=1)
        v_r = lax.dynamic_slice_in_dim(v_g, r_start, kh, axis=1)
        dr = r_start + jnp.arange(kh) - r + NA_ROWS - 1
        bias = rpb[:, dr][:, :, dc].astype(jnp.float32)
        bias = jnp.where(col_ok[None, None], bias, NEG_INF).transpose(0, 2, 1, 3)
        s_win = jnp.einsum('bqhd,brkhd->bhqrk', q_r, k_r,
                           preferred_element_type=jnp.float32) * scale + bias[None]
        s_ctx = jnp.einsum('bqhd,bchd->bhqc', q_r, kc,
                           preferred_element_type=jnp.float32) * scale
        s = jnp.concatenate([s_win.reshape(bsz, N_HEADS, GRID_W, kh * GRID_W), s_ctx], axis=-1)
        p = jax.nn.softmax(s, axis=-1).astype(v.dtype)
        p_win = p[..., :kh * GRID_W].reshape(bsz, N_HEADS, GRID_W, kh, GRID_W)
        p_ctx = p[..., kh * GRID_W:]
        return (jnp.einsum('bhqrk,brkhd->bqhd', p_win, v_r)
                + jnp.einsum('bhqc,bchd->bqhd', p_ctx, vc))

    o = lax.map(row_block, jnp.arange(rows))
    o = jnp.moveaxis(o, 0, 1).reshape(bsz, n, D_MODEL)
    y = o @ w_out
    if not with_ctx_out:
        return y, None
    sc = jnp.einsum('bqhd,bkhd->bhqk', qc, kc, preferred_element_type=jnp.float32) * scale
    pc = jax.nn.softmax(sc, axis=-1).astype(vc.dtype)
    oc = jnp.einsum('bhqk,bkhd->bqhd', pc, vc).reshape(bsz, hc.shape[1], D_MODEL)
    return y, oc @ w_out


def setup_inputs(seed: int = 0) -> dict:
    key = jax.random.key(seed)
    ks = jax.random.split(key, 18)
    D = D_MODEL
    n_conv = (DEPTH + 1) // 2
    n_attn = DEPTH // 2

    def nrm(k, shape):
        return jax.random.normal(k, shape, jnp.float32)

    def lin(k, shape, fan_in, gain=1.0):
        return nrm(k, shape) * (gain * fan_in ** -0.5)

    def norm_gain(k, shape):
        return 1.0 + 0.05 * nrm(k, shape)

    return {
        "x": nrm(ks[0], (BATCH, SEQ, D)),
        "c": nrm(ks[1], (BATCH, D)),
        "ctx": nrm(ks[2], (BATCH, CTX_LEN, D)),
        "c_ctx": nrm(ks[3], (D,)),
        "w_ada": lin(ks[4], (DEPTH, D, 6 * D), D, 0.5),
        "b_ada": 0.02 * nrm(ks[5], (DEPTH, 6 * D)),
        "norm_mix": norm_gain(ks[6], (DEPTH, D)),
        "norm_ffn": norm_gain(ks[7], (DEPTH, D)),
        "conv_w_in": lin(ks[8], (n_conv, D, 3 * D), D),
        "conv_w": lin(ks[9], (n_conv, CONV_WIDTH, D), CONV_WIDTH),
        "conv_w_out": lin(ks[10], (n_conv, D, D), D),
        "attn_w_qkv": lin(ks[11], (n_attn, D, 3 * D), D),
        "attn_q_norm": norm_gain(ks[12], (n_attn, HEAD_DIM)),
        "attn_k_norm": norm_gain(ks[13], (n_attn, HEAD_DIM)),
        "attn_rpb": 0.02 * nrm(ks[14], (n_attn, N_HEADS, 2 * NA_ROWS - 1, 2 * NA_COLS - 1)),
        "attn_w_out": lin(ks[15], (n_attn, D, D), D),
        "ffn_w_in": lin(ks[16], (DEPTH, D, 2 * FFN_HIDDEN), D),
        "ffn_w_out": lin(ks[17], (DEPTH, FFN_HIDDEN, D), FFN_HIDDEN),
    }


def reference(x, c, ctx, c_ctx, w_ada, b_ada, norm_mix, norm_ffn,
              conv_w_in, conv_w, conv_w_out,
              attn_w_qkv, attn_q_norm, attn_k_norm, attn_rpb, attn_w_out,
              ffn_w_in, ffn_w_out):
    for i in range(DEPTH):
        update_ctx = i < DEPTH - 1
        j = i // 2
        mod = jax.nn.silu(c) @ w_ada[i] + b_ada[i]
        mod_c = jax.nn.silu(c_ctx) @ w_ada[i] + b_ada[i]
        sh1, sc1, g1, sh2, sc2, g2 = jnp.split(mod[:, None, :], 6, axis=-1)
        sh1c, sc1c, g1c, sh2c, sc2c, g2c = jnp.split(mod_c, 6, axis=-1)

        h = _modulate(_rms_norm(x, norm_mix[i]), sh1, sc1)
        hc = _modulate(_rms_norm(ctx, norm_mix[i]), sh1c, sc1c)
        if i % 2 == 0:
            y = _short_conv_mixer(h, conv_w_in[j], conv_w[j], conv_w_out[j])
            yc = _short_conv_mixer(hc, conv_w_in[j], conv_w[j], conv_w_out[j]) if update_ctx else None
        else:
            y, yc = _neighbourhood_attention(h, hc, attn_w_qkv[j], attn_q_norm[j], attn_k_norm[j],
                                             attn_rpb[j], attn_w_out[j], update_ctx)
        x = x + g1 * y
        x = x + g2 * _swiglu(_modulate(_rms_norm(x, norm_ffn[i]), sh2, sc2), ffn_w_in[i], ffn_w_out[i])
        if update_ctx:
            ctx = ctx + g1c * yc
            ctx = ctx + g2c * _swiglu(_modulate(_rms_norm(ctx, norm_ffn[i]), sh2c, sc2c),
                                      ffn_w_in[i], ffn_w_out[i])
    return x
```

```python
import functools

import jax
import jax.numpy as jnp
from jax import lax
from jax.experimental import pallas as pl
from jax.experimental.pallas import tpu as pltpu

D_MODEL = 1024
DEPTH = 4
GRID_W = 64
N_HEADS = 16
HEAD_DIM = D_MODEL // N_HEADS
NA_ROWS = 8
NA_COLS = 16
FFN_HIDDEN = 2816
EPS = 1e-6
NEG_INF = -1e30

F32 = jnp.float32
BF16 = jnp.bfloat16

VMEM_LIMIT_BYTES = 56 * 1024 * 1024
HEADS_PER_GROUP = 4
GROUP_W = HEADS_PER_GROUP * HEAD_DIM
N_GROUPS = N_HEADS // HEADS_PER_GROUP
ROWS_PER_STEP = 8
KEY_ROWS_PER_STEP = 2 * NA_ROWS


def _params(*sem):
    return pltpu.CompilerParams(dimension_semantics=sem,
                                vmem_limit_bytes=VMEM_LIMIT_BYTES)


def _const_spec(shape):
    nd = len(shape)
    return pl.BlockSpec(shape, lambda *_: (0,) * nd, pipeline_mode=pl.Buffered(1))


def _mod_spec(n_mod_rows):
    if n_mod_rows == 1:
        return pl.BlockSpec((1, 6, D_MODEL), lambda b, i: (0, 0, 0))
    return pl.BlockSpec((1, 6, D_MODEL), lambda b, i: (b, 0, 0))


def _silu(x):
    return x / (1.0 + jnp.exp(-x))


def _norm_mod(x, gain_scale, shift):
    ms = jnp.mean(x * x, axis=-1, keepdims=True)
    return x * lax.rsqrt(ms + EPS) * gain_scale + shift


def _dot(a, b):
    return jnp.dot(a, b, preferred_element_type=F32)


def _dot_nt(a, b):
    return lax.dot_general(a, b, (((1,), (1,)), ((), ())),
                           preferred_element_type=F32)


ADA_COLS = 1024


def _ada_kernel(c_ref, w_ref, b_ref, o_ref):
    a = _silu(c_ref[...]).astype(BF16)
    o_ref[0] = _dot(a, w_ref[0].astype(BF16)) + b_ref[0]


def _ada(cond, w_ada, b_ada):
    n_rows = cond.shape[0]
    n_cols = w_ada.shape[-1]
    return pl.pallas_call(
        _ada_kernel,
        grid=(DEPTH, n_cols // ADA_COLS),
        in_specs=[
            pl.BlockSpec((n_rows, D_MODEL), lambda l, j: (0, 0)),
            pl.BlockSpec((1, D_MODEL, ADA_COLS), lambda l, j: (l, 0, j)),
            pl.BlockSpec((1, 1, ADA_COLS), lambda l, j: (l, 0, j)),
        ],
        out_specs=pl.BlockSpec((1, n_rows, ADA_COLS), lambda l, j: (l, 0, j)),
        out_shape=jax.ShapeDtypeStruct((DEPTH, n_rows, n_cols), F32),
        compiler_params=_params("arbitrary", "arbitrary"),
        name="ada",
    )(cond, w_ada, b_ada.reshape(DEPTH, 1, n_cols))


CONV_CHUNK = 256
HALO = 8


def _conv_kernel(x_ref, xp_ref, xn_ref, mod_ref, gain_ref, win_ref, cw_ref,
                 wout_ref, o_ref, *, tm):
    i = pl.program_id(1)
    last = pl.num_programs(1) - 1
    m = mod_ref[0]
    shift = m[0:1]
    gain_scale = gain_ref[...] * (1.0 + m[1:2])
    gate = m[2:3]

    x = x_ref[0]
    h = _norm_mod(x, gain_scale, shift).astype(BF16)

    x_halo = jnp.concatenate([xp_ref[0], xn_ref[0]], axis=0)
    h_halo = _norm_mod(x_halo, gain_scale, shift).astype(BF16)
    cu_halo = _dot(h_halo, win_ref[:, D_MODEL:])
    z_halo = cu_halo[:, :D_MODEL] * cu_halo[:, D_MODEL:]
    z_prev = jnp.where(i > 0, z_halo[HALO - 1:HALO], 0.0)
    z_next = jnp.where(i < last, z_halo[HALO:HALO + 1], 0.0)

    cw = cw_ref[...]
    rows = lax.broadcasted_iota(jnp.int32, (tm, CONV_CHUNK), 0)
    acc = None
    for j in range(D_MODEL // CONV_CHUNK):
        lo, hi = j * CONV_CHUNK, (j + 1) * CONV_CHUNK
        b_gate = _dot(h, win_ref[:, lo:hi])
        c_gate = _dot(h, win_ref[:, D_MODEL + lo:D_MODEL + hi])
        u = _dot(h, win_ref[:, 2 * D_MODEL + lo:2 * D_MODEL + hi])
        z = c_gate * u
        z_up = jnp.where(rows == 0, z_prev[:, lo:hi], pltpu.roll(z, 1, 0))
        z_dn = jnp.where(rows == tm - 1, z_next[:, lo:hi], pltpu.roll(z, tm - 1, 0))
        zc = cw[0:1, lo:hi] * z_up + cw[1:2, lo:hi] * z + cw[2:3, lo:hi] * z_dn
        y = _dot((b_gate * zc).astype(BF16), wout_ref[lo:hi, :])
        acc = y if acc is None else acc + y
    o_ref[0] = x + gate * acc


def _conv_mixer(x, mod, gain, w_in, conv_w, w_out, *, tm):
    bsz, n, _ = x.shape
    nt = n // tm
    hb = tm // HALO
    n_hb = n // HALO
    return pl.pallas_call(
        functools.partial(_conv_kernel, tm=tm),
        grid=(bsz, nt),
        in_specs=[
            pl.BlockSpec((1, tm, D_MODEL), lambda b, i: (b, i, 0)),
            pl.BlockSpec((1, HALO, D_MODEL),
                         lambda b, i: (b, jnp.maximum(i * hb - 1, 0), 0)),
            pl.BlockSpec((1, HALO, D_MODEL),
                         lambda b, i: (b, jnp.minimum((i + 1) * hb, n_hb - 1), 0)),
            _mod_spec(mod.shape[0]),
            _const_spec((1, D_MODEL)),
            _const_spec((D_MODEL, 3 * D_MODEL)),
            _const_spec((3, D_MODEL)),
            _const_spec((D_MODEL, D_MODEL)),
        ],
        out_specs=pl.BlockSpec((1, tm, D_MODEL), lambda b, i: (b, i, 0)),
        out_shape=jax.ShapeDtypeStruct(x.shape, F32),
        compiler_params=_params("parallel", "arbitrary"),
        name="conv_mixer",
    )(x, x, x, mod, gain, w_in, conv_w, w_out)


FFN_CHUNK = 256


def _ffn_kernel(x_ref, mod_ref, gain_ref, win_ref, wout_ref, o_ref):
    m = mod_ref[0]
    shift = m[3:4]
    gain_scale = gain_ref[...] * (1.0 + m[4:5])
    gate = m[5:6]
    x = x_ref[0]
    h = _norm_mod(x, gain_scale, shift).astype(BF16)
    acc = None
    for j in range(FFN_HIDDEN // FFN_CHUNK):
        lo, hi = j * FFN_CHUNK, (j + 1) * FFN_CHUNK
        g = _dot(h, win_ref[:, lo:hi])
        u = _dot(h, win_ref[:, FFN_HIDDEN + lo:FFN_HIDDEN + hi])
        y = _dot((_silu(g) * u).astype(BF16), wout_ref[lo:hi, :])
        acc = y if acc is None else acc + y
    o_ref[0] = x + gate * acc


def _ffn(x, mod, gain, w_in, w_out, *, tm):
    bsz, n, _ = x.shape
    return pl.pallas_call(
        _ffn_kernel,
        grid=(bsz, n // tm),
        in_specs=[
            pl.BlockSpec((1, tm, D_MODEL), lambda b, i: (b, i, 0)),
            _mod_spec(mod.shape[0]),
            _const_spec((1, D_MODEL)),
            _const_spec((D_MODEL, 2 * FFN_HIDDEN)),
            _const_spec((FFN_HIDDEN, D_MODEL)),
        ],
        out_specs=pl.BlockSpec((1, tm, D_MODEL), lambda b, i: (b, i, 0)),
        out_shape=jax.ShapeDtypeStruct(x.shape, F32),
        compiler_params=_params("parallel", "arbitrary"),
        name="ffn",
    )(x, mod, gain, w_in, w_out)


def _head_mean_matrix():
    r = lax.broadcasted_iota(jnp.int32, (GROUP_W, GROUP_W), 0) // HEAD_DIM
    c = lax.broadcasted_iota(jnp.int32, (GROUP_W, GROUP_W), 1) // HEAD_DIM
    return jnp.where(r == c, 1.0, 0.0).astype(BF16)


def _qkv_kernel(x_ref, mod_ref, gain_ref, w_ref, gq_ref, gk_ref,
                q_ref, k_ref, v_ref):
    m = mod_ref[0]
    shift = m[0:1]
    gain_scale = gain_ref[...] * (1.0 + m[1:2])
    h = _norm_mod(x_ref[0], gain_scale, shift).astype(BF16)
    ones_bd = _head_mean_matrix()
    gq = gq_ref[...] * (HEAD_DIM ** -0.5)
    gk = gk_ref[...]
    for j in range(N_GROUPS):
        lo, hi = j * GROUP_W, (j + 1) * GROUP_W
        for base, g, out in ((0, gq, q_ref), (D_MODEL, gk, k_ref)):
            t = _dot(h, w_ref[:, base + lo:base + hi])
            ms = _dot((t * t).astype(BF16), ones_bd) * (1.0 / HEAD_DIM)
            out[0, :, lo:hi] = (t * lax.rsqrt(ms + EPS) * g).astype(BF16)
        v_ref[0, :, lo:hi] = _dot(
            h, w_ref[:, 2 * D_MODEL + lo:2 * D_MODEL + hi]).astype(BF16)


def _qkv(x, mod, gain, w_qkv, gq, gk, *, tm):
    bsz, n, _ = x.shape
    tok = pl.BlockSpec((1, tm, D_MODEL), lambda b, i: (b, i, 0))
    out = jax.ShapeDtypeStruct(x.shape, BF16)
    return pl.pallas_call(
        _qkv_kernel,
        grid=(bsz, n // tm),
        in_specs=[
            tok,
            _mod_spec(mod.shape[0]),
            _const_spec((1, D_MODEL)),
            _const_spec((D_MODEL, 3 * D_MODEL)),
            _const_spec((1, GROUP_W)),
            _const_spec((1, GROUP_W)),
        ],
        out_specs=[tok, tok, tok],
        out_shape=[out, out, out],
        compiler_params=_params("parallel", "arbitrary"),
        name="qkv",
    )(x, mod, gain, w_qkv, gq, gk)


def _stack_heads(q):
    lane_head = lax.broadcasted_iota(jnp.int32, q.shape, 1) // HEAD_DIM
    zero = jnp.zeros_like(q)
    return jnp.concatenate(
        [jnp.where(lane_head == hh, q, zero) for hh in range(HEADS_PER_GROUP)], axis=0)


def _unstack_heads(o, n):
    lane_head = lax.broadcasted_iota(jnp.int32, (n, GROUP_W), 1) // HEAD_DIM
    out = o[0:n]
    for hh in range(1, HEADS_PER_GROUP):
        out = jnp.where(lane_head == hh, o[hh * n:(hh + 1) * n], out)
    return out


def _nbr_attn_kernel(q_ref, k_ref, v_ref, kc_ref, vc_ref, bias_ref, o_ref):
    rows = pl.num_programs(1) * ROWS_PER_STEP
    r0 = pl.program_id(1) * ROWS_PER_STEP
    win0 = jnp.clip(r0 - NA_ROWS // 2, 0, rows - KEY_ROWS_PER_STEP)
    n_win = NA_ROWS * GRID_W

    def row_body(i, carry):
        r = r0 + i
        r_start = jnp.clip(r - NA_ROWS // 2, 0, rows - NA_ROWS)
        cls = r - r_start
        koff = pl.multiple_of((r_start - win0) * GRID_W, GRID_W)
        qoff = pl.multiple_of(i * GRID_W, GRID_W)
        for g in range(N_GROUPS):
            lo, hi = g * GROUP_W, (g + 1) * GROUP_W
            q = _stack_heads(q_ref[0, pl.ds(qoff, GRID_W), lo:hi])
            s_win = _dot_nt(q, k_ref[0, pl.ds(koff, n_win), lo:hi]) + bias_ref[cls, g]
            s_ctx = _dot_nt(q, kc_ref[0, :, lo:hi])
            mx = jnp.maximum(jnp.max(s_win, axis=-1, keepdims=True),
                             jnp.max(s_ctx, axis=-1, keepdims=True))
            p_win = jnp.exp(s_win - mx)
            p_ctx = jnp.exp(s_ctx - mx)
            denom = (jnp.sum(p_win, axis=-1, keepdims=True)
                     + jnp.sum(p_ctx, axis=-1, keepdims=True))
            pv = (_dot(p_win.astype(BF16), v_ref[0, pl.ds(koff, n_win), lo:hi])
                  + _dot(p_ctx.astype(BF16), vc_ref[0, :, lo:hi]))
            o = _unstack_heads(pv / denom, GRID_W)
            o_ref[0, pl.ds(qoff, GRID_W), lo:hi] = o.astype(BF16)
        return carry

    lax.fori_loop(0, ROWS_PER_STEP, row_body, 0)


def _nbr_attn(q, k, v, kc, vc, bias):
    bsz, n, _ = q.shape
    rows = n // GRID_W
    tq = ROWS_PER_STEP * GRID_W
    tk = KEY_ROWS_PER_STEP * GRID_W
    n_ctx = kc.shape[1]

    def key_map(b, i):
        start = jnp.clip(i * ROWS_PER_STEP - NA_ROWS // 2, 0, rows - KEY_ROWS_PER_STEP)
        return (b, start * GRID_W, 0)

    key_spec = pl.BlockSpec((pl.Element(1), pl.Element(tk), pl.Element(D_MODEL)), key_map)
    ctx_spec = pl.BlockSpec((1, n_ctx, D_MODEL), lambda b, i: (b, 0, 0))
    return pl.pallas_call(
        _nbr_attn_kernel,
        grid=(bsz, rows // ROWS_PER_STEP),
        in_specs=[
            pl.BlockSpec((1, tq, D_MODEL), lambda b, i: (b, i, 0)),
            key_spec, key_spec, ctx_spec, ctx_spec,
            _const_spec(bias.shape),
        ],
        out_specs=pl.BlockSpec((1, tq, D_MODEL), lambda b, i: (b, i, 0)),
        out_shape=jax.ShapeDtypeStruct(q.shape, BF16),
        compiler_params=_params("parallel", "arbitrary"),
        name="nbr_attn",
    )(q, k, v, kc, vc, bias)


def _ctx_attn_kernel(q_ref, k_ref, v_ref, o_ref):
    n = q_ref.shape[1]
    q = _stack_heads(q_ref[0])
    s = _dot_nt(q, k_ref[0])
    p = jnp.exp(s - jnp.max(s, axis=-1, keepdims=True))
    denom = jnp.sum(p, axis=-1, keepdims=True)
    pv = _dot(p.astype(BF16), v_ref[0])
    o_ref[0] = _unstack_heads(pv / denom, n).astype(BF16)


def _ctx_attn(q, k, v):
    bsz, n, _ = q.shape
    spec = pl.BlockSpec((1, n, GROUP_W), lambda b, g: (b, 0, g))
    return pl.pallas_call(
        _ctx_attn_kernel,
        grid=(bsz, N_GROUPS),
        in_specs=[spec, spec, spec],
        out_specs=spec,
        out_shape=jax.ShapeDtypeStruct(q.shape, BF16),
        compiler_params=_params("parallel", "arbitrary"),
        name="ctx_attn",
    )(q, k, v)


def _out_proj_kernel(x_ref, o_ref, mod_ref, w_ref, y_ref):
    gate = mod_ref[0][2:3]
    y_ref[0] = x_ref[0] + gate * _dot(o_ref[0], w_ref[...])


def _out_proj(x, o, mod, w_out, *, tm):
    bsz, n, _ = x.shape
    tok = pl.BlockSpec((1, tm, D_MODEL), lambda b, i: (b, i, 0))
    return pl.pallas_call(
        _out_proj_kernel,
        grid=(bsz, n // tm),
        in_specs=[tok, tok, _mod_spec(mod.shape[0]), _const_spec((D_MODEL, D_MODEL))],
        out_specs=tok,
        out_shape=jax.ShapeDtypeStruct(x.shape, F32),
        compiler_params=_params("parallel", "arbitrary"),
        name="out_proj",
    )(x, o, mod, w_out)


def _bias_table(rpb):
    col = jnp.arange(GRID_W)
    col_start = jnp.clip(col - NA_COLS // 2, 0, GRID_W - NA_COLS)
    col_ok = (col[None, :] >= col_start[:, None]) & (col[None, :] < col_start[:, None] + NA_COLS)
    dc = jnp.clip(col[None, :] - col[:, None] + NA_COLS - 1, 0, 2 * NA_COLS - 2)
    b15 = jnp.where(col_ok[None, None], rpb[:, :, dc].astype(F32), NEG_INF)
    dr = jnp.arange(NA_ROWS)[None, :] - jnp.arange(NA_ROWS)[:, None] + NA_ROWS - 1
    t = b15[:, dr]
    t = t.transpose(1, 0, 3, 2, 4)
    return t.reshape(NA_ROWS, N_GROUPS, HEADS_PER_GROUP * GRID_W, NA_ROWS * GRID_W)


LATENT_TILE = 512


def kernel(x, c, ctx, c_ctx, w_ada, b_ada, norm_mix, norm_ffn, conv_w_in, conv_w,
           conv_w_out, attn_w_qkv, attn_q_norm, attn_k_norm, attn_rpb, attn_w_out,
           ffn_w_in, ffn_w_out):
    bsz, n, d = x.shape
    n_ctx = ctx.shape[1]
    assert d == D_MODEL and n % (ROWS_PER_STEP * GRID_W) == 0 and n % LATENT_TILE == 0
    assert n // GRID_W >= KEY_ROWS_PER_STEP and n_ctx % HALO == 0

    cond = jnp.concatenate(
        [c, c_ctx[None], jnp.zeros((16 - bsz - 1, d), F32)], axis=0)
    mod = _ada(cond, w_ada, b_ada).reshape(DEPTH, 16, 6, d)

    for i in range(DEPTH):
        update_ctx = i < DEPTH - 1
        j = i // 2
        mod_x = mod[i, :bsz]
        mod_c = mod[i, bsz:bsz + 1]
        g_mix = norm_mix[i][None]
        g_ffn = norm_ffn[i][None]
        if i % 2 == 0:
            w_in = conv_w_in[j].astype(BF16)
            w_out = conv_w_out[j].astype(BF16)
            x = _conv_mixer(x, mod_x, g_mix, w_in, conv_w[j], w_out, tm=LATENT_TILE)
            if update_ctx:
                ctx = _conv_mixer(ctx, mod_c, g_mix, w_in, conv_w[j], w_out, tm=n_ctx)
        else:
            w_qkv = attn_w_qkv[j].astype(BF16)
            w_out = attn_w_out[j].astype(BF16)
            gq = jnp.tile(attn_q_norm[j], HEADS_PER_GROUP)[None]
            gk = jnp.tile(attn_k_norm[j], HEADS_PER_GROUP)[None]
            q, k, v = _qkv(x, mod_x, g_mix, w_qkv, gq, gk, tm=LATENT_TILE)
            qc, kc, vc = _qkv(ctx, mod_c, g_mix, w_qkv, gq, gk, tm=n_ctx)
            o = _nbr_attn(q, k, v, kc, vc, _bias_table(attn_rpb[j]))
            x = _out_proj(x, o, mod_x, w_out, tm=LATENT_TILE)
            if update_ctx:
                oc = _ctx_attn(qc, kc, vc)
                ctx = _out_proj(ctx, oc, mod_c, w_out, tm=n_ctx)
        w_in = ffn_w_in[i].astype(BF16)
        w_out = ffn_w_out[i].astype(BF16)
        x = _ffn(x, mod_x, g_ffn, w_in, w_out, tm=LATENT_TILE)
        if update_ctx:
            ctx = _ffn(ctx, mod_c, g_ffn, w_in, w_out, tm=n_ctx)
    return x
```

```python
import functools

import jax
import jax.numpy as jnp
from jax import lax
from jax.experimental import pallas as pl
from jax.experimental.pallas import tpu as pltpu

D_MODEL = 1024
DEPTH = 4
GRID_W = 64
N_HEADS = 16
HEAD_DIM = D_MODEL // N_HEADS
NA_ROWS = 8
NA_COLS = 16
FFN_HIDDEN = 2816
EPS = 1e-6
NEG_INF = -1e30

F32 = jnp.float32
BF16 = jnp.bfloat16

VMEM_LIMIT_BYTES = 56 * 1024 * 1024
HEADS_PER_GROUP = 4
GROUP_W = HEADS_PER_GROUP * HEAD_DIM
N_GROUPS = N_HEADS // HEADS_PER_GROUP
ROWS_PER_STEP = 8
KEY_ROWS_PER_STEP = 2 * NA_ROWS


def _params(*sem):
    return pltpu.CompilerParams(dimension_semantics=sem,
                                vmem_limit_bytes=VMEM_LIMIT_BYTES)


def _const_spec(shape):
    nd = len(shape)
    return pl.BlockSpec(shape, lambda *_: (0,) * nd, pipeline_mode=pl.Buffered(1))


def _mod_spec(n_mod_rows):
    if n_mod_rows == 1:
        return pl.BlockSpec((1, 6, D_MODEL), lambda b, i: (0, 0, 0))
    return pl.BlockSpec((1, 6, D_MODEL), lambda b, i: (b, 0, 0))


def _silu(x):
    return x / (1.0 + jnp.exp(-x))


def _norm_mod(x, gain_scale, shift):
    ms = jnp.mean(x * x, axis=-1, keepdims=True)
    return x * lax.rsqrt(ms + EPS) * gain_scale + shift


def _dot(a, b):
    return jnp.dot(a, b, preferred_element_type=F32)


def _dot_nt(a, b):
    return lax.dot_general(a, b, (((1,), (1,)), ((), ())),
                           preferred_element_type=F32)


ADA_COLS = 1024


def _ada_kernel(c_ref, w_ref, b_ref, o_ref):
    a = _silu(c_ref[...]).astype(BF16)
    o_ref[0] = _dot(a, w_ref[0].astype(BF16)) + b_ref[0]


def _ada(cond, w_ada, b_ada):
    n_rows = cond.shape[0]
    n_cols = w_ada.shape[-1]
    return pl.pallas_call(
        _ada_kernel,
        grid=(DEPTH, n_cols // ADA_COLS),
        in_specs=[
            pl.BlockSpec((n_rows, D_MODEL), lambda l, j: (0, 0)),
            pl.BlockSpec((1, D_MODEL, ADA_COLS), lambda l, j: (l, 0, j)),
            pl.BlockSpec((1, 1, ADA_COLS), lambda l, j: (l, 0, j)),
        ],
        out_specs=pl.BlockSpec((1, n_rows, ADA_COLS), lambda l, j: (l, 0, j)),
        out_shape=jax.ShapeDtypeStruct((DEPTH, n_rows, n_cols), F32),
        compiler_params=_params("arbitrary", "arbitrary"),
        name="ada",
    )(cond, w_ada, b_ada.reshape(DEPTH, 1, n_cols))


CONV_CHUNK = 512
HALO = 8


def _conv_kernel(x_ref, xp_ref, xn_ref, mod_ref, gain_ref, win_ref, cw_ref,
                 wout_ref, o_ref, *, tm):
    i = pl.program_id(1)
    last = pl.num_programs(1) - 1
    m = mod_ref[0]
    shift = m[0:1]
    gain_scale = gain_ref[...] * (1.0 + m[1:2])
    gate = m[2:3]

    x = x_ref[0]
    x_ext = jnp.concatenate([x, xp_ref[0], xn_ref[0]], axis=0)
    h_ext = _norm_mod(x_ext, gain_scale, shift).astype(BF16)
    h = h_ext[:tm]
    has_prev = i > 0
    has_next = i < last

    cw = cw_ref[...]
    rows = lax.broadcasted_iota(jnp.int32, (tm, CONV_CHUNK), 0)
    acc = None
    for j in range(D_MODEL // CONV_CHUNK):
        lo, hi = j * CONV_CHUNK, (j + 1) * CONV_CHUNK
        b_gate = _dot(h, win_ref[:, lo:hi])
        c_gate = _dot(h_ext, win_ref[:, D_MODEL + lo:D_MODEL + hi])
        u = _dot(h_ext, win_ref[:, 2 * D_MODEL + lo:2 * D_MODEL + hi])
        z_ext = c_gate * u
        z = z_ext[:tm]
        z_prev = jnp.where(has_prev, z_ext[tm + HALO - 1:tm + HALO], 0.0)
        z_next = jnp.where(has_next, z_ext[tm + HALO:tm + HALO + 1], 0.0)
        z_up = jnp.where(rows == 0, z_prev, pltpu.roll(z, 1, 0))
        z_dn = jnp.where(rows == tm - 1, z_next, pltpu.roll(z, tm - 1, 0))
        zc = cw[0:1, lo:hi] * z_up + cw[1:2, lo:hi] * z + cw[2:3, lo:hi] * z_dn
        y = _dot((b_gate * zc).astype(BF16), wout_ref[lo:hi, :])
        acc = y if acc is None else acc + y
    o_ref[0] = x + gate * acc


def _conv_mixer(x, mod, gain, w_in, conv_w, w_out, *, tm):
    bsz, n, _ = x.shape
    nt = n // tm
    hb = tm // HALO
    n_hb = n // HALO
    return pl.pallas_call(
        functools.partial(_conv_kernel, tm=tm),
        grid=(bsz, nt),
        in_specs=[
            pl.BlockSpec((1, tm, D_MODEL), lambda b, i: (b, i, 0)),
            pl.BlockSpec((1, HALO, D_MODEL),
                         lambda b, i: (b, jnp.maximum(i * hb - 1, 0), 0)),
            pl.BlockSpec((1, HALO, D_MODEL),
                         lambda b, i: (b, jnp.minimum((i + 1) * hb, n_hb - 1), 0)),
            _mod_spec(mod.shape[0]),
            _const_spec((1, D_MODEL)),
            _const_spec((D_MODEL, 3 * D_MODEL)),
            _const_spec((3, D_MODEL)),
            _const_spec((D_MODEL, D_MODEL)),
        ],
        out_specs=pl.BlockSpec((1, tm, D_MODEL), lambda b, i: (b, i, 0)),
        out_shape=jax.ShapeDtypeStruct(x.shape, F32),
        compiler_params=_params("parallel", "arbitrary"),
        name="conv_mixer",
    )(x, x, x, mod, gain, w_in, conv_w, w_out)


FFN_CHUNK = 256


def _ffn_kernel(*refs, with_attn_out):
    if with_attn_out:
        x_ref, ao_ref, wo_ref, mod_ref, gain_ref, win_ref, wout_ref, o_ref = refs
    else:
        x_ref, mod_ref, gain_ref, win_ref, wout_ref, o_ref = refs
    m = mod_ref[0]
    shift = m[3:4]
    gain_scale = gain_ref[...] * (1.0 + m[4:5])
    gate = m[5:6]
    x = x_ref[0]
    if with_attn_out:
        x = x + m[2:3] * _dot(ao_ref[0], wo_ref[...])
    h = _norm_mod(x, gain_scale, shift).astype(BF16)
    acc = None
    for j in range(FFN_HIDDEN // FFN_CHUNK):
        lo, hi = j * FFN_CHUNK, (j + 1) * FFN_CHUNK
        g = _dot(h, win_ref[:, lo:hi])
        u = _dot(h, win_ref[:, FFN_HIDDEN + lo:FFN_HIDDEN + hi])
        y = _dot((_silu(g) * u).astype(BF16), wout_ref[lo:hi, :])
        acc = y if acc is None else acc + y
    o_ref[0] = x + gate * acc


def _ffn(x, mod, gain, w_in, w_out, *, tm, attn_o=None, w_o=None):
    bsz, n, _ = x.shape
    tok = pl.BlockSpec((1, tm, D_MODEL), lambda b, i: (b, i, 0))
    with_attn_out = attn_o is not None
    attn_args = (attn_o, w_o) if with_attn_out else ()
    attn_specs = [tok, _const_spec((D_MODEL, D_MODEL))] if with_attn_out else []
    return pl.pallas_call(
        functools.partial(_ffn_kernel, with_attn_out=with_attn_out),
        grid=(bsz, n // tm),
        in_specs=[tok] + attn_specs + [
            _mod_spec(mod.shape[0]),
            _const_spec((1, D_MODEL)),
            _const_spec((D_MODEL, 2 * FFN_HIDDEN)),
            _const_spec((FFN_HIDDEN, D_MODEL)),
        ],
        out_specs=tok,
        out_shape=jax.ShapeDtypeStruct(x.shape, F32),
        compiler_params=_params("parallel", "arbitrary"),
        name="attn_out_ffn" if with_attn_out else "ffn",
    )(x, *attn_args, mod, gain, w_in, w_out)


QKV_CHUNK = 512


def _head_mean_matrix():
    r = lax.broadcasted_iota(jnp.int32, (GROUP_W, GROUP_W), 0) // HEAD_DIM
    c = lax.broadcasted_iota(jnp.int32, (GROUP_W, GROUP_W), 1) // HEAD_DIM
    return jnp.where(r == c, 1.0, 0.0).astype(BF16)


def _qkv_kernel(x_ref, mod_ref, gain_ref, w_ref, gq_ref, gk_ref,
                q_ref, k_ref, v_ref):
    m = mod_ref[0]
    shift = m[0:1]
    gain_scale = gain_ref[...] * (1.0 + m[1:2])
    h = _norm_mod(x_ref[0], gain_scale, shift).astype(BF16)
    ones_bd = _head_mean_matrix()
    gq = gq_ref[...] * (HEAD_DIM ** -0.5)
    gk = gk_ref[...]
    for j in range(D_MODEL // QKV_CHUNK):
        lo = j * QKV_CHUNK
        for base, g, out in ((0, gq, q_ref), (D_MODEL, gk, k_ref)):
            t2 = _dot(h, w_ref[:, base + lo:base + lo + QKV_CHUNK])
            for s in range(QKV_CHUNK // GROUP_W):
                t = t2[:, s * GROUP_W:(s + 1) * GROUP_W]
                ms = _dot((t * t).astype(BF16), ones_bd) * (1.0 / HEAD_DIM)
                c0 = lo + s * GROUP_W
                out[0, :, c0:c0 + GROUP_W] = (t * lax.rsqrt(ms + EPS) * g).astype(BF16)
        v_ref[0, :, lo:lo + QKV_CHUNK] = _dot(
            h, w_ref[:, 2 * D_MODEL + lo:2 * D_MODEL + lo + QKV_CHUNK]).astype(BF16)


def _qkv(x, mod, gain, w_qkv, gq, gk, *, tm):
    bsz, n, _ = x.shape
    tok = pl.BlockSpec((1, tm, D_MODEL), lambda b, i: (b, i, 0))
    out = jax.ShapeDtypeStruct(x.shape, BF16)
    return pl.pallas_call(
        _qkv_kernel,
        grid=(bsz, n // tm),
        in_specs=[
            tok,
            _mod_spec(mod.shape[0]),
            _const_spec((1, D_MODEL)),
            _const_spec((D_MODEL, 3 * D_MODEL)),
            _const_spec((1, GROUP_W)),
            _const_spec((1, GROUP_W)),
        ],
        out_specs=[tok, tok, tok],
        out_shape=[out, out, out],
        compiler_params=_params("parallel", "arbitrary"),
        name="qkv",
    )(x, mod, gain, w_qkv, gq, gk)


def _stack_heads(q):
    lane_head = lax.broadcasted_iota(jnp.int32, q.shape, 1) // HEAD_DIM
    zero = jnp.zeros_like(q)
    return jnp.concatenate(
        [jnp.where(lane_head == hh, q, zero) for hh in range(HEADS_PER_GROUP)], axis=0)


def _unstack_heads(o, n):
    lane_head = lax.broadcasted_iota(jnp.int32, (n, GROUP_W), 1) // HEAD_DIM
    out = o[0:n]
    for hh in range(1, HEADS_PER_GROUP):
        out = jnp.where(lane_head == hh, o[hh * n:(hh + 1) * n], out)
    return out


def _nbr_attn_kernel(q_ref, k_ref, v_ref, kc_ref, vc_ref, bias_ref, o_ref):
    rows = pl.num_programs(1) * ROWS_PER_STEP
    r0 = pl.program_id(1) * ROWS_PER_STEP
    win0 = jnp.clip(r0 - NA_ROWS // 2, 0, rows - KEY_ROWS_PER_STEP)
    n_win = NA_ROWS * GRID_W

    for i in range(ROWS_PER_STEP):
        r = r0 + i
        r_start = jnp.clip(r - NA_ROWS // 2, 0, rows - NA_ROWS)
        cls = r - r_start
        koff = pl.multiple_of((r_start - win0) * GRID_W, GRID_W)
        qoff = i * GRID_W
        for g in range(N_GROUPS):
            lo, hi = g * GROUP_W, (g + 1) * GROUP_W
            q = _stack_heads(q_ref[0, pl.ds(qoff, GRID_W), lo:hi])
            s_win = _dot_nt(q, k_ref[0, pl.ds(koff, n_win), lo:hi]) + bias_ref[cls, g]
            s_ctx = _dot_nt(q, kc_ref[0, :, lo:hi])
            mx = jnp.maximum(jnp.max(s_win, axis=-1, keepdims=True),
                             jnp.max(s_ctx, axis=-1, keepdims=True))
            p_win = jnp.exp(s_win - mx)
            p_ctx = jnp.exp(s_ctx - mx)
            denom = (jnp.sum(p_win, axis=-1, keepdims=True)
                     + jnp.sum(p_ctx, axis=-1, keepdims=True))
            pv = (_dot(p_win.astype(BF16), v_ref[0, pl.ds(koff, n_win), lo:hi])
                  + _dot(p_ctx.astype(BF16), vc_ref[0, :, lo:hi]))
            o = _unstack_heads(pv / denom, GRID_W)
            o_ref[0, pl.ds(qoff, GRID_W), lo:hi] = o.astype(BF16)


def _nbr_attn(q, k, v, kc, vc, bias):
    bsz, n, _ = q.shape
    rows = n // GRID_W
    tq = ROWS_PER_STEP * GRID_W
    tk = KEY_ROWS_PER_STEP * GRID_W
    n_ctx = kc.shape[1]

    def key_map(b, i):
        start = jnp.clip(i * ROWS_PER_STEP - NA_ROWS // 2, 0, rows - KEY_ROWS_PER_STEP)
        return (b, start * GRID_W, 0)

    key_spec = pl.BlockSpec((pl.Element(1), pl.Element(tk), pl.Element(D_MODEL)), key_map)
    ctx_spec = pl.BlockSpec((1, n_ctx, D_MODEL), lambda b, i: (b, 0, 0))
    return pl.pallas_call(
        _nbr_attn_kernel,
        grid=(bsz, rows // ROWS_PER_STEP),
        in_specs=[
            pl.BlockSpec((1, tq, D_MODEL), lambda b, i: (b, i, 0)),
            key_spec, key_spec, ctx_spec, ctx_spec,
            _const_spec(bias.shape),
        ],
        out_specs=pl.BlockSpec((1, tq, D_MODEL), lambda b, i: (b, i, 0)),
        out_shape=jax.ShapeDtypeStruct(q.shape, BF16),
        compiler_params=_params("parallel", "arbitrary"),
        name="nbr_attn",
    )(q, k, v, kc, vc, bias)


def _ctx_attn_kernel(q_ref, k_ref, v_ref, o_ref):
    n = q_ref.shape[1]
    q = _stack_heads(q_ref[0])
    s = _dot_nt(q, k_ref[0])
    p = jnp.exp(s - jnp.max(s, axis=-1, keepdims=True))
    denom = jnp.sum(p, axis=-1, keepdims=True)
    pv = _dot(p.astype(BF16), v_ref[0])
    o_ref[0] = _unstack_heads(pv / denom, n).astype(BF16)


def _ctx_attn(q, k, v):
    bsz, n, _ = q.shape
    spec = pl.BlockSpec((1, n, GROUP_W), lambda b, g: (b, 0, g))
    return pl.pallas_call(
        _ctx_attn_kernel,
        grid=(bsz, N_GROUPS),
        in_specs=[spec, spec, spec],
        out_specs=spec,
        out_shape=jax.ShapeDtypeStruct(q.shape, BF16),
        compiler_params=_params("parallel", "arbitrary"),
        name="ctx_attn",
    )(q, k, v)


def _bias_table(rpb):
    col = jnp.arange(GRID_W)
    col_start = jnp.clip(col - NA_COLS // 2, 0, GRID_W - NA_COLS)
    col_ok = (col[None, :] >= col_start[:, None]) & (col[None, :] < col_start[:, None] + NA_COLS)
    dc = jnp.clip(col[None, :] - col[:, None] + NA_COLS - 1, 0, 2 * NA_COLS - 2)
    b15 = jnp.where(col_ok[None, None], rpb[:, :, dc].astype(F32), NEG_INF)
    dr = jnp.arange(NA_ROWS)[None, :] - jnp.arange(NA_ROWS)[:, None] + NA_ROWS - 1
    t = b15[:, dr]
    t = t.transpose(1, 0, 3, 2, 4)
    return t.reshape(NA_ROWS, N_GROUPS, HEADS_PER_GROUP * GRID_W, NA_ROWS * GRID_W)


LATENT_TILE = 512
CTX_TILE = 512


def kernel(x, c, ctx, c_ctx, w_ada, b_ada, norm_mix, norm_ffn, conv_w_in, conv_w,
           conv_w_out, attn_w_qkv, attn_q_norm, attn_k_norm, attn_rpb, attn_w_out,
           ffn_w_in, ffn_w_out):
    bsz, n, d = x.shape
    n_ctx = ctx.shape[1]
    assert d == D_MODEL and n % (ROWS_PER_STEP * GRID_W) == 0 and n % LATENT_TILE == 0
    assert n // GRID_W >= KEY_ROWS_PER_STEP and n_ctx % HALO == 0
    assert (bsz * n_ctx) % CTX_TILE == 0
    flat = (1, bsz * n_ctx, d)
    per_batch = (bsz, n_ctx, d)

    cond =jnp.concatenate(
        [c, c_ctx[None], jnp.zeros((16 - bsz - 1, d), F32)], axis=0)
    mod = _ada(cond, w_ada, b_ada).reshape(DEPTH, 16, 6, d)

    for i in range(DEPTH):
        update_ctx = i < DEPTH - 1
        j = i // 2
        mod_x = mod[i, :bsz]
        mod_c = mod[i, bsz:bsz + 1]
        g_mix = norm_mix[i][None]
        g_ffn = norm_ffn[i][None]
        w_in = ffn_w_in[i].astype(BF16)
        w_out = ffn_w_out[i].astype(BF16)
        if i % 2 == 0:
            cw_in = conv_w_in[j].astype(BF16)
            cw_out = conv_w_out[j].astype(BF16)
            x = _conv_mixer(x, mod_x, g_mix, cw_in, conv_w[j], cw_out, tm=LATENT_TILE)
            x = _ffn(x, mod_x, g_ffn, w_in, w_out, tm=LATENT_TILE)
            if update_ctx:
                ctx = _conv_mixer(ctx, mod_c, g_mix, cw_in, conv_w[j], cw_out, tm=n_ctx)
                ctx = _ffn(ctx.reshape(flat), mod_c, g_ffn, w_in, w_out,
                           tm=CTX_TILE).reshape(per_batch)
        else:
            w_qkv = attn_w_qkv[j].astype(BF16)
            w_o = attn_w_out[j].astype(BF16)
            gq = jnp.tile(attn_q_norm[j], HEADS_PER_GROUP)[None]
            gk = jnp.tile(attn_k_norm[j], HEADS_PER_GROUP)[None]
            q, k, v = _qkv(x, mod_x, g_mix, w_qkv, gq, gk, tm=LATENT_TILE)
            qc, kc, vc = (t.reshape(per_batch) for t in _qkv(
                ctx.reshape(flat), mod_c, g_mix, w_qkv, gq, gk, tm=CTX_TILE))
            o = _nbr_attn(q, k, v, kc, vc, _bias_table(attn_rpb[j]))
            x = _ffn(x, mod_x, g_ffn, w_in, w_out, tm=LATENT_TILE, attn_o=o, w_o=w_o)
            if update_ctx:
                oc = _ctx_attn(qc, kc, vc)
                ctx = _ffn(ctx.reshape(flat), mod_c, g_ffn, w_in, w_out, tm=CTX_TILE,
                           attn_o=oc.reshape(flat), w_o=w_o).reshape(per_batch)
    return x
```

```python
import functools

import jax
import jax.numpy as jnp
from jax import lax
from jax.experimental import pallas as pl
from jax.experimental.pallas import tpu as pltpu

D_MODEL = 1024
DEPTH = 4
GRID_W = 64
N_HEADS = 16
HEAD_DIM = D_MODEL // N_HEADS
NA_ROWS = 8
NA_COLS = 16
FFN_HIDDEN = 2816
EPS = 1e-6
NEG_INF = -1e30

F32 = jnp.float32
BF16 = jnp.bfloat16

VMEM_LIMIT_BYTES = 56 * 1024 * 1024
HEADS_PER_GROUP = 4
GROUP_W = HEADS_PER_GROUP * HEAD_DIM
N_GROUPS = N_HEADS // HEADS_PER_GROUP
ROWS_PER_STEP = 8
KEY_ROWS_PER_STEP = 2 * NA_ROWS


def _params(*sem):
    return pltpu.CompilerParams(dimension_semantics=sem,
                                vmem_limit_bytes=VMEM_LIMIT_BYTES)


def _const_spec(shape):
    nd = len(shape)
    return pl.BlockSpec(shape, lambda *_: (0,) * nd, pipeline_mode=pl.Buffered(1))


def _layer_spec(shape, layer):
    nd = len(shape)
    return pl.BlockSpec((None,) + tuple(shape), lambda *_: (layer,) + (0,) * nd,
                        pipeline_mode=pl.Buffered(1))


def _mod_spec(layer, cond_row):
    if cond_row is None:
        return pl.BlockSpec((None, None, 6, D_MODEL), lambda b, i: (layer, b, 0, 0))
    return pl.BlockSpec((None, None, 6, D_MODEL), lambda b, i: (layer, cond_row, 0, 0))


def _silu(x):
    return x / (1.0 + jnp.exp(-x))


def _norm_mod(x, gain_scale, shift):
    ms = jnp.mean(x * x, axis=-1, keepdims=True)
    return x * lax.rsqrt(ms + EPS) * gain_scale + shift


def _dot(a, b):
    return jnp.dot(a, b, preferred_element_type=F32)


def _dot_nt(a, b):
    return lax.dot_general(a, b, (((1,), (1,)), ((), ())),
                           preferred_element_type=F32)


ADA_COLS = 1024


def _ada_kernel(c_ref, w_ref, b_ref, o_ref):
    a = _silu(c_ref[...]).astype(BF16)
    o_ref[0] = _dot(a, w_ref[0].astype(BF16)) + b_ref[0]


def _ada(cond, w_ada, b_ada):
    n_rows = cond.shape[0]
    n_cols = w_ada.shape[-1]
    return pl.pallas_call(
        _ada_kernel,
        grid=(DEPTH, n_cols // ADA_COLS),
        in_specs=[
            pl.BlockSpec((n_rows, D_MODEL), lambda l, j: (0, 0)),
            pl.BlockSpec((1, D_MODEL, ADA_COLS), lambda l, j: (l, 0, j)),
            pl.BlockSpec((1, 1, ADA_COLS), lambda l, j: (l, 0, j)),
        ],
        out_specs=pl.BlockSpec((1, n_rows, ADA_COLS), lambda l, j: (l, 0, j)),
        out_shape=jax.ShapeDtypeStruct((DEPTH, n_rows, n_cols), F32),
        compiler_params=_params("arbitrary", "arbitrary"),
        name="ada",
    )(cond, w_ada, b_ada.reshape(DEPTH, 1, n_cols))


CONV_CHUNK = 512
HALO = 8


def _conv_kernel(x_ref, xp_ref, xn_ref, mod_ref, gain_ref, win_ref, cw_ref,
                 wout_ref, o_ref, *, tm):
    i = pl.program_id(1)
    last = pl.num_programs(1) - 1
    m = mod_ref[...]
    shift = m[0:1]
    gain_scale = gain_ref[...] * (1.0 + m[1:2])
    gate = m[2:3]

    x = x_ref[0]
    x_ext = jnp.concatenate([x, xp_ref[0], xn_ref[0]], axis=0)
    h_ext = _norm_mod(x_ext, gain_scale, shift).astype(BF16)
    h = h_ext[:tm]
    has_prev = i > 0
    has_next = i < last

    cw = cw_ref[...]
    rows = lax.broadcasted_iota(jnp.int32, (tm, CONV_CHUNK), 0)
    acc = None
    for j in range(D_MODEL // CONV_CHUNK):
        lo, hi = j * CONV_CHUNK, (j + 1) * CONV_CHUNK
        b_gate = _dot(h, win_ref[:, lo:hi])
        c_gate = _dot(h_ext, win_ref[:, D_MODEL + lo:D_MODEL + hi])
        u = _dot(h_ext, win_ref[:, 2 * D_MODEL + lo:2 * D_MODEL + hi])
        z_ext = c_gate * u
        z = z_ext[:tm]
        z_prev = jnp.where(has_prev, z_ext[tm + HALO - 1:tm + HALO], 0.0)
        z_next = jnp.where(has_next, z_ext[tm + HALO:tm + HALO + 1], 0.0)
        z_up = jnp.where(rows == 0, z_prev, pltpu.roll(z, 1, 0))
        z_dn = jnp.where(rows == tm - 1, z_next, pltpu.roll(z, tm - 1, 0))
        zc = cw[0:1, lo:hi] * z_up + cw[1:2, lo:hi] * z + cw[2:3, lo:hi] * z_dn
        y = _dot((b_gate * zc).astype(BF16), wout_ref[lo:hi, :])
        acc = y if acc is None else acc + y
    o_ref[0] = x + gate * acc


def _conv_mixer(x, mod, gain, w_in, conv_w, w_out, *, tm, layer, mixer, cond_row):
    bsz, n, _ = x.shape
    nt = n // tm
    hb = tm // HALO
    n_hb = n // HALO
    return pl.pallas_call(
        functools.partial(_conv_kernel, tm=tm),
        grid=(bsz, nt),
        in_specs=[
            pl.BlockSpec((1, tm, D_MODEL), lambda b, i: (b, i, 0)),
            pl.BlockSpec((1, HALO, D_MODEL),
                         lambda b, i: (b, jnp.maximum(i * hb - 1, 0), 0)),
            pl.BlockSpec((1, HALO, D_MODEL),
                         lambda b, i: (b, jnp.minimum((i + 1) * hb, n_hb - 1), 0)),
            _mod_spec(layer, cond_row),
            _layer_spec((1, D_MODEL), layer),
            _layer_spec((D_MODEL, 3 * D_MODEL), mixer),
            _layer_spec((3, D_MODEL), mixer),
            _layer_spec((D_MODEL, D_MODEL), mixer),
        ],
        out_specs=pl.BlockSpec((1, tm, D_MODEL), lambda b, i: (b, i, 0)),
        out_shape=jax.ShapeDtypeStruct(x.shape, F32),
        compiler_params=_params("parallel", "arbitrary"),
        name="conv_mixer",
    )(x, x, x, mod, gain, w_in, conv_w, w_out)


FFN_CHUNK = 256


def _ffn_kernel(*refs, with_attn_out):
    if with_attn_out:
        x_ref, ao_ref, wo_ref, mod_ref, gain_ref, win_ref, wout_ref, o_ref = refs
    else:
        x_ref, mod_ref, gain_ref, win_ref, wout_ref, o_ref = refs
    m = mod_ref[...]
    shift = m[3:4]
    gain_scale = gain_ref[...] * (1.0 + m[4:5])
    gate = m[5:6]
    x = x_ref[0]
    if with_attn_out:
        x = x + m[2:3] * _dot(ao_ref[0], wo_ref[...])
    h = _norm_mod(x, gain_scale, shift).astype(BF16)
    acc = None
    for j in range(FFN_HIDDEN // FFN_CHUNK):
        lo, hi = j * FFN_CHUNK, (j + 1) * FFN_CHUNK
        g = _dot(h, win_ref[:, lo:hi])
        u = _dot(h, win_ref[:, FFN_HIDDEN + lo:FFN_HIDDEN + hi])
        y = _dot((_silu(g) * u).astype(BF16), wout_ref[lo:hi, :])
        acc = y if acc is None else acc + y
    o_ref[0] = x + gate * acc


def _ffn(x, mod, gain, w_in, w_out, *, tm, layer, cond_row, attn_o=None, w_o=None,
         mixer=None):
    bsz, n, _ = x.shape
    tok = pl.BlockSpec((1, tm, D_MODEL), lambda b, i: (b, i, 0))
    with_attn_out = attn_o is not None
    attn_args = (attn_o, w_o) if with_attn_out else ()
    attn_specs = [tok, _layer_spec((D_MODEL, D_MODEL), mixer)] if with_attn_out else []
    return pl.pallas_call(
        functools.partial(_ffn_kernel, with_attn_out=with_attn_out),
        grid=(bsz, n // tm),
        in_specs=[tok] + attn_specs + [
            _mod_spec(layer, cond_row),
            _layer_spec((1, D_MODEL), layer),
            _layer_spec((D_MODEL, 2 * FFN_HIDDEN), layer),
            _layer_spec((FFN_HIDDEN, D_MODEL), layer),
        ],
        out_specs=tok,
        out_shape=jax.ShapeDtypeStruct(x.shape, F32),
        compiler_params=_params("parallel", "arbitrary"),
        name="attn_out_ffn" if with_attn_out else "ffn",
    )(x, *attn_args, mod, gain, w_in, w_out)


QKV_CHUNK = 512


def _head_mean_matrix():
    r = lax.broadcasted_iota(jnp.int32, (GROUP_W, GROUP_W), 0) // HEAD_DIM
    c = lax.broadcasted_iota(jnp.int32, (GROUP_W, GROUP_W), 1) // HEAD_DIM
    return jnp.where(r == c, 1.0, 0.0).astype(BF16)


def _qkv_kernel(x_ref, mod_ref, gain_ref, w_ref, gq_ref, gk_ref,
                q_ref, k_ref, v_ref):
    m = mod_ref[...]
    shift = m[0:1]
    gain_scale = gain_ref[...] * (1.0 + m[1:2])
    h = _norm_mod(x_ref[0], gain_scale, shift).astype(BF16)
    ones_bd = _head_mean_matrix()
    gq = gq_ref[...] * (HEAD_DIM ** -0.5)
    gk = gk_ref[...]
    for j in range(D_MODEL // QKV_CHUNK):
        lo = j * QKV_CHUNK
        for base, g, out in ((0, gq, q_ref), (D_MODEL, gk, k_ref)):
            t2 = _dot(h, w_ref[:, base + lo:base + lo + QKV_CHUNK])
            for s in range(QKV_CHUNK // GROUP_W):
                t = t2[:, s * GROUP_W:(s + 1) * GROUP_W]
                ms = _dot((t * t).astype(BF16), ones_bd) * (1.0 / HEAD_DIM)
                c0 = lo + s * GROUP_W
                out[0, :, c0:c0 + GROUP_W] = (t * lax.rsqrt(ms + EPS) * g).astype(BF16)
        v_ref[0, :, lo:lo + QKV_CHUNK] = _dot(
            h, w_ref[:, 2 * D_MODEL + lo:2 * D_MODEL + lo + QKV_CHUNK]).astype(BF16)


def _qkv(x, mod, gain, w_qkv, gq, gk, *, tm, layer, mixer, cond_row):
    bsz, n, _ = x.shape
    tok = pl.BlockSpec((1, tm, D_MODEL), lambda b, i: (b, i, 0))
    out = jax.ShapeDtypeStruct(x.shape, BF16)
    return pl.pallas_call(
        _qkv_kernel,
        grid=(bsz, n // tm),
        in_specs=[
            tok,
            _mod_spec(layer, cond_row),
            _layer_spec((1, D_MODEL), layer),
            _layer_spec((D_MODEL, 3 * D_MODEL), mixer),
            _layer_spec((1, GROUP_W), mixer),
            _layer_spec((1, GROUP_W), mixer),
        ],
        out_specs=[tok, tok, tok],
        out_shape=[out, out, out],
        compiler_params=_params("parallel", "arbitrary"),
        name="qkv",
    )(x, mod, gain, w_qkv, gq, gk)


def _stack_heads(q):
    lane_head = lax.broadcasted_iota(jnp.int32, q.shape, 1) // HEAD_DIM
    zero = jnp.zeros_like(q)
    return jnp.concatenate(
        [jnp.where(lane_head == hh, q, zero) for hh in range(HEADS_PER_GROUP)], axis=0)


def _unstack_heads(o, n):
    lane_head = lax.broadcasted_iota(jnp.int32, (n, GROUP_W), 1) // HEAD_DIM
    out = o[0:n]
    for hh in range(1, HEADS_PER_GROUP):
        out = jnp.where(lane_head == hh, o[hh * n:(hh + 1) * n], out)
    return out


def _nbr_attn_kernel(q_ref, k_ref, v_ref, kc_ref, vc_ref, bias_ref, o_ref):
    rows = pl.num_programs(1) * ROWS_PER_STEP
    r0 = pl.program_id(1) * ROWS_PER_STEP
    win0 = jnp.clip(r0 - NA_ROWS // 2, 0, rows - KEY_ROWS_PER_STEP)
    n_win = NA_ROWS * GRID_W

    for i in range(ROWS_PER_STEP):
        r = r0 + i
        r_start = jnp.clip(r - NA_ROWS // 2, 0, rows - NA_ROWS)
        cls = r - r_start
        koff = pl.multiple_of((r_start - win0) * GRID_W, GRID_W)
        qoff = i * GRID_W
        for g in range(N_GROUPS):
            lo, hi = g * GROUP_W, (g + 1) * GROUP_W
            q = _stack_heads(q_ref[0, pl.ds(qoff, GRID_W), lo:hi])
            s_win = _dot_nt(q, k_ref[0, pl.ds(koff, n_win), lo:hi]) + bias_ref[cls, g]
            s_ctx = _dot_nt(q, kc_ref[0, :, lo:hi])
            mx = jnp.maximum(jnp.max(s_win, axis=-1, keepdims=True),
                             jnp.max(s_ctx, axis=-1, keepdims=True))
            p_win = jnp.exp(s_win - mx)
            p_ctx = jnp.exp(s_ctx - mx)
            denom = (jnp.sum(p_win, axis=-1, keepdims=True)
                     + jnp.sum(p_ctx, axis=-1, keepdims=True))
            pv = (_dot(p_win.astype(BF16), v_ref[0, pl.ds(koff, n_win), lo:hi])
                  + _dot(p_ctx.astype(BF16), vc_ref[0, :, lo:hi]))
            o = _unstack_heads(pv / denom, GRID_W)
            o_ref[0, pl.ds(qoff, GRID_W), lo:hi] = o.astype(BF16)


def _nbr_attn(q, k, v, kc, vc, bias):
    bsz, n, _ = q.shape
    rows = n // GRID_W
    tq = ROWS_PER_STEP * GRID_W
    tk = KEY_ROWS_PER_STEP * GRID_W
    n_ctx = kc.shape[1]

    def key_map(b, i):
        start = jnp.clip(i * ROWS_PER_STEP - NA_ROWS // 2, 0, rows - KEY_ROWS_PER_STEP)
        return (b, start * GRID_W, 0)

    key_spec = pl.BlockSpec((pl.Element(1), pl.Element(tk), pl.Element(D_MODEL)), key_map)
    ctx_spec = pl.BlockSpec((1, n_ctx, D_MODEL), lambda b, i: (b, 0, 0))
    return pl.pallas_call(
        _nbr_attn_kernel,
        grid=(bsz, rows // ROWS_PER_STEP),
        in_specs=[
            pl.BlockSpec((1, tq, D_MODEL), lambda b, i: (b, i, 0)),
            key_spec, key_spec, ctx_spec, ctx_spec,
            _const_spec(bias.shape),
        ],
        out_specs=pl.BlockSpec((1, tq, D_MODEL), lambda b, i: (b, i, 0)),
        out_shape=jax.ShapeDtypeStruct(q.shape, BF16),
        compiler_params=_params("parallel", "arbitrary"),
        name="nbr_attn",
    )(q, k, v, kc, vc, bias)


def _ctx_attn_kernel(q_ref, k_ref, v_ref, o_ref):
    n = q_ref.shape[1]
    q = _stack_heads(q_ref[0])
    s = _dot_nt(q, k_ref[0])
    p = jnp.exp(s - jnp.max(s, axis=-1, keepdims=True))
    denom = jnp.sum(p, axis=-1, keepdims=True)
    pv = _dot(p.astype(BF16), v_ref[0])
    o_ref[0] = _unstack_heads(pv / denom, n).astype(BF16)


def _ctx_attn(q, k, v):
    bsz, n, _ = q.shape
    spec = pl.BlockSpec((1, n, GROUP_W), lambda b, g: (b, 0, g))
    return pl.pallas_call(
        _ctx_attn_kernel,
        grid=(bsz, N_GROUPS),
        in_specs=[spec, spec, spec],
        out_specs=spec,
        out_shape=jax.ShapeDtypeStruct(q.shape, BF16),
        compiler_params=_params("parallel", "arbitrary"),
        name="ctx_attn",
    )(q, k, v)


def _bias_table(rpb):
    col = jnp.arange(GRID_W)
    col_start = jnp.clip(col - NA_COLS // 2, 0, GRID_W - NA_COLS)
    col_ok = (col[None, :] >= col_start[:, None]) & (col[None, :] < col_start[:, None] + NA_COLS)
    dc = jnp.clip(col[None, :] - col[:, None] + NA_COLS - 1, 0, 2 * NA_COLS - 2)
    onehot = (dc[None] == jnp.arange(2 * NA_COLS - 1)[:, None, None]).astype(F32)
    b15 = jnp.einsum("hrc,cqk->hrqk", rpb.astype(F32), onehot,
                     precision=lax.Precision.HIGHEST)
    b15 = jnp.where(col_ok[None, None], b15, NEG_INF)
    t = jnp.stack([b15[:, NA_ROWS - 1 - cls:2 * NA_ROWS - 1 - cls]
                   for cls in range(NA_ROWS)])
    t = t.transpose(0, 1, 3, 2, 4)
    return t.reshape(NA_ROWS, N_GROUPS, HEADS_PER_GROUP * GRID_W, NA_ROWS * GRID_W)


LATENT_TILE = 512
CTX_TILE = 512
COND_ROWS = 16


def kernel(x, c, ctx, c_ctx, w_ada, b_ada, norm_mix, norm_ffn, conv_w_in, conv_w,
           conv_w_out, attn_w_qkv, attn_q_norm, attn_k_norm, attn_rpb, attn_w_out,
           ffn_w_in, ffn_w_out):
    bsz, n, d = x.shape
    n_ctx = ctx.shape[1]
    assert d == D_MODEL and n % (ROWS_PER_STEP * GRID_W) == 0 and n % LATENT_TILE == 0
    assert n // GRID_W >= KEY_ROWS_PER_STEP and n_ctx % HALO == 0
    assert (bsz * n_ctx) % CTX_TILE == 0 and bsz < COND_ROWS
    flat = (1, bsz * n_ctx, d)
    per_batch = (bsz, n_ctx, d)
    ctx_row = bsz

    cond = jnp.concatenate(
        [c, c_ctx[None], jnp.zeros((COND_ROWS - bsz - 1, d), F32)], axis=0)
    mod = _ada(cond, w_ada, b_ada).reshape(DEPTH, COND_ROWS, 6, d)

    g_mix = norm_mix.reshape(DEPTH, 1, d)
    g_ffn = norm_ffn.reshape(DEPTH, 1, d)
    f_in = ffn_w_in.astype(BF16)
    f_out = ffn_w_out.astype(BF16)
    c_in = conv_w_in.astype(BF16)
    c_out = conv_w_out.astype(BF16)
    a_qkv = attn_w_qkv.astype(BF16)
    a_out = attn_w_out.astype(BF16)
    gq = jnp.tile(attn_q_norm, (1, HEADS_PER_GROUP))[:, None]
    gk = jnp.tile(attn_k_norm, (1, HEADS_PER_GROUP))[:, None]

    for i in range(DEPTH):
        update_ctx = i < DEPTH - 1
        j = i // 2
        lat = dict(layer=i, cond_row=None)
        con = dict(layer=i, cond_row=ctx_row)
        if i % 2 == 0:
            x = _conv_mixer(x, mod, g_mix, c_in, conv_w, c_out, tm=LATENT_TILE, mixer=j, **lat)
            x = _ffn(x, mod, g_ffn, f_in, f_out, tm=LATENT_TILE, **lat)
            if update_ctx:
                ctx = _conv_mixer(ctx, mod, g_mix, c_in, conv_w, c_out, tm=n_ctx, mixer=j, **con)
                ctx = _ffn(ctx.reshape(flat), mod, g_ffn, f_in, f_out, tm=CTX_TILE,
                           **con).reshape(per_batch)
        else:
            q, k, v = _qkv(x, mod, g_mix, a_qkv, gq, gk, tm=LATENT_TILE, mixer=j, **lat)
            qc, kc, vc = (t.reshape(per_batch) for t in _qkv(
                ctx.reshape(flat), mod, g_mix, a_qkv, gq, gk, tm=CTX_TILE, mixer=j, **con))
            o = _nbr_attn(q, k, v, kc, vc, _bias_table(attn_rpb[j]))
            x = _ffn(x, mod, g_ffn, f_in, f_out, tm=LATENT_TILE, attn_o=o, w_o=a_out,
                     mixer=j, **lat)
            if update_ctx:
                oc = _ctx_attn(qc, kc, vc)
                ctx = _ffn(ctx.reshape(flat), mod, g_ffn, f_in, f_out, tm=CTX_TILE,
                           attn_o=oc.reshape(flat), w_o=a_out, mixer=j, **con).reshape(per_batch)
    return x
```

```python
import functools

import jax
import jax.numpy as jnp
from jax import lax
from jax.experimental import pallas as pl
from jax.experimental.pallas import tpu as pltpu

D_MODEL = 1024
DEPTH = 4
GRID_W = 64
N_HEADS = 16
HEAD_DIM = D_MODEL // N_HEADS
NA_ROWS = 8
NA_COLS = 16
FFN_HIDDEN = 2816
EPS = 1e-6
NEG_INF = -1e30

F32 = jnp.float32
BF16 = jnp.bfloat16

VMEM_LIMIT_BYTES = 56 * 1024 * 1024
HEADS_PER_GROUP = 4
GROUP_W = HEADS_PER_GROUP * HEAD_DIM
N_GROUPS = N_HEADS // HEADS_PER_GROUP
ROWS_PER_STEP = 8
KEY_ROWS_PER_STEP = 2 * NA_ROWS
COL_BLOCK = NA_COLS
N_COL_BLOCKS = GRID_W // COL_BLOCK
WIN_BLOCK = NA_ROWS * COL_BLOCK
QUERY_TILE = 8


def _query_segments():
    segs = []
    for q0 in range(0, GRID_W, QUERY_TILE):
        first = min(max(q0 - NA_COLS // 2, 0), GRID_W - NA_COLS)
        last = min(max(q0 + QUERY_TILE - 1 - NA_COLS // 2, 0), GRID_W - NA_COLS) + NA_COLS
        kb0 = min(first // COL_BLOCK, N_COL_BLOCKS - 2)
        assert last <= (kb0 + 2) * COL_BLOCK
        if segs and segs[-1][2] == kb0:
            segs[-1] = (segs[-1][0], q0 + QUERY_TILE, kb0)
        else:
            segs.append((q0, q0 + QUERY_TILE, kb0))
    return tuple(segs)


QUERY_SEGMENTS = _query_segments()
SEG_KEYS = 2 * WIN_BLOCK


def _params(*sem):
    return pltpu.CompilerParams(dimension_semantics=sem,
                                vmem_limit_bytes=VMEM_LIMIT_BYTES)


def _const_spec(shape):
    nd = len(shape)
    return pl.BlockSpec(shape, lambda *_: (0,) * nd, pipeline_mode=pl.Buffered(1))


def _layer_spec(shape, layer):
    nd = len(shape)
    return pl.BlockSpec((None,) + tuple(shape), lambda *_: (layer,) + (0,) * nd,
                        pipeline_mode=pl.Buffered(1))


def _mod_spec(layer, cond_row):
    if cond_row is None:
        return pl.BlockSpec((None, None, 6, D_MODEL), lambda b, i: (layer, b, 0, 0))
    return pl.BlockSpec((None, None, 6, D_MODEL), lambda b, i: (layer, cond_row, 0, 0))


def _silu(x):
    return x / (1.0 + jnp.exp(-x))


def _norm_mod(x, gain_scale, shift):
    ms = jnp.mean(x * x, axis=-1, keepdims=True)
    return x * lax.rsqrt(ms + EPS) * gain_scale + shift


def _dot(a, b):
    return jnp.dot(a, b, preferred_element_type=F32)


def _dot_nt(a, b):
    return lax.dot_general(a, b, (((1,), (1,)), ((), ())),
                           preferred_element_type=F32)


ADA_COLS = 1024


def _ada_kernel(c_ref, w_ref, b_ref, o_ref):
    a = _silu(c_ref[...]).astype(BF16)
    o_ref[0] = _dot(a, w_ref[0].astype(BF16)) + b_ref[0]


def _ada(cond, w_ada, b_ada):
    n_rows = cond.shape[0]
    n_cols = w_ada.shape[-1]
    return pl.pallas_call(
        _ada_kernel,
        grid=(DEPTH, n_cols // ADA_COLS),
        in_specs=[
            pl.BlockSpec((n_rows, D_MODEL), lambda l, j: (0, 0)),
            pl.BlockSpec((1, D_MODEL, ADA_COLS), lambda l, j: (l, 0, j)),
            pl.BlockSpec((1, 1, ADA_COLS), lambda l, j: (l, 0, j)),
        ],
        out_specs=pl.BlockSpec((1, n_rows, ADA_COLS), lambda l, j: (l, 0, j)),
        out_shape=jax.ShapeDtypeStruct((DEPTH, n_rows, n_cols), F32),
        compiler_params=_params("arbitrary", "arbitrary"),
        name="ada",
    )(cond, w_ada, b_ada.reshape(DEPTH, 1, n_cols))


CONV_CHUNK = 512
HALO = 8


def _conv_kernel(x_ref, xp_ref, xn_ref, mod_ref, gain_ref, win_ref, cw_ref,
                 wout_ref, o_ref, *, tm):
    i = pl.program_id(1)
    last = pl.num_programs(1) - 1
    m = mod_ref[...]
    shift = m[0:1]
    gain_scale = gain_ref[...] * (1.0 + m[1:2])
    gate = m[2:3]

    x = x_ref[0]
    x_ext = jnp.concatenate([x, xp_ref[0], xn_ref[0]], axis=0)
    h_ext = _norm_mod(x_ext, gain_scale, shift).astype(BF16)
    h = h_ext[:tm]
    has_prev = i > 0
    has_next = i < last

    cw = cw_ref[...]
    rows = lax.broadcasted_iota(jnp.int32, (tm, CONV_CHUNK), 0)
    acc = None
    for j in range(D_MODEL // CONV_CHUNK):
        lo, hi = j * CONV_CHUNK, (j + 1) * CONV_CHUNK
        b_gate = _dot(h, win_ref[:, lo:hi])
        c_gate = _dot(h_ext, win_ref[:, D_MODEL + lo:D_MODEL + hi])
        u = _dot(h_ext, win_ref[:, 2 * D_MODEL + lo:2 * D_MODEL + hi])
        z_ext = c_gate * u
        z = z_ext[:tm]
        z_prev = jnp.where(has_prev, z_ext[tm + HALO - 1:tm + HALO], 0.0)
        z_next = jnp.where(has_next, z_ext[tm + HALO:tm + HALO + 1], 0.0)
        z_up = jnp.where(rows == 0, z_prev, pltpu.roll(z, 1, 0))
        z_dn = jnp.where(rows == tm - 1, z_next, pltpu.roll(z, tm - 1, 0))
        zc = cw[0:1, lo:hi] * z_up + cw[1:2, lo:hi] * z + cw[2:3, lo:hi] * z_dn
        y = _dot((b_gate * zc).astype(BF16), wout_ref[lo:hi, :])
        acc = y if acc is None else acc + y
    o_ref[0] = x + gate * acc


def _conv_mixer(x, mod, gain, w_in, conv_w, w_out, *, tm, layer, mixer, cond_row):
    bsz, n, _ = x.shape
    nt = n // tm
    hb = tm // HALO
    n_hb = n // HALO
    return pl.pallas_call(
        functools.partial(_conv_kernel, tm=tm),
        grid=(bsz, nt),
        in_specs=[
            pl.BlockSpec((1, tm, D_MODEL), lambda b, i: (b, i, 0)),
            pl.BlockSpec((1, HALO, D_MODEL),
                         lambda b, i: (b, jnp.maximum(i * hb - 1, 0), 0)),
            pl.BlockSpec((1, HALO, D_MODEL),
                         lambda b, i: (b, jnp.minimum((i + 1) * hb, n_hb - 1), 0)),
            _mod_spec(layer, cond_row),
            _layer_spec((1, D_MODEL), layer),
            _layer_spec((D_MODEL, 3 * D_MODEL), mixer),
            _layer_spec((3, D_MODEL), mixer),
            _layer_spec((D_MODEL, D_MODEL), mixer),
        ],
        out_specs=pl.BlockSpec((1, tm, D_MODEL), lambda b, i: (b, i, 0)),
        out_shape=jax.ShapeDtypeStruct(x.shape, F32),
        compiler_params=_params("parallel", "arbitrary"),
        name="conv_mixer",
    )(x, x, x, mod, gain, w_in, conv_w, w_out)


FFN_CHUNK = 256


def _ffn_kernel(*refs, with_attn_out):
    if with_attn_out:
        x_ref, ao_ref, wo_ref, mod_ref, gain_ref, win_ref, wout_ref, o_ref = refs
    else:
        x_ref, mod_ref, gain_ref, win_ref, wout_ref, o_ref = refs
    m = mod_ref[...]
    shift = m[3:4]
    gain_scale = gain_ref[...] * (1.0 + m[4:5])
    gate = m[5:6]
    x = x_ref[0]
    if with_attn_out:
        x = x + m[2:3] * _dot(ao_ref[0], wo_ref[...])
    h = _norm_mod(x, gain_scale, shift).astype(BF16)
    acc = None
    for j in range(FFN_HIDDEN // FFN_CHUNK):
        lo, hi = j * FFN_CHUNK, (j + 1) * FFN_CHUNK
        g = _dot(h, win_ref[:, lo:hi])
        u = _dot(h, win_ref[:, FFN_HIDDEN + lo:FFN_HIDDEN + hi])
        y = _dot((_silu(g) * u).astype(BF16), wout_ref[lo:hi, :])
        acc = y if acc is None else acc + y
    o_ref[0] = x + gate * acc


def _ffn(x, mod, gain, w_in, w_out, *, tm, layer, cond_row, attn_o=None, w_o=None,
         mixer=None):
    bsz, n, _ = x.shape
    tok = pl.BlockSpec((1, tm, D_MODEL), lambda b, i: (b, i, 0))
    with_attn_out = attn_o is not None
    attn_args = (attn_o, w_o) if with_attn_out else ()
    attn_specs = [tok, _layer_spec((D_MODEL, D_MODEL), mixer)] if with_attn_out else []
    return pl.pallas_call(
        functools.partial(_ffn_kernel, with_attn_out=with_attn_out),
        grid=(bsz, n // tm),
        in_specs=[tok] + attn_specs + [
            _mod_spec(layer, cond_row),
            _layer_spec((1, D_MODEL), layer),
            _layer_spec((D_MODEL, 2 * FFN_HIDDEN), layer),
            _layer_spec((FFN_HIDDEN, D_MODEL), layer),
        ],
        out_specs=tok,
        out_shape=jax.ShapeDtypeStruct(x.shape, F32),
        compiler_params=_params("parallel", "arbitrary"),
        name="attn_out_ffn" if with_attn_out else "ffn",
    )(x, *attn_args, mod, gain, w_in, w_out)


QKV_CHUNK = 512


def _head_mean_matrix():
    r = lax.broadcasted_iota(jnp.int32, (GROUP_W, GROUP_W), 0) // HEAD_DIM
    c = lax.broadcasted_iota(jnp.int32, (GROUP_W, GROUP_W), 1) // HEAD_DIM
    return jnp.where(r == c, 1.0, 0.0).astype(BF16)


def _store_keys(ref, val, c0, by_col_block):
    tm, w = val.shape
    if not by_col_block:
        ref[0, :, c0:c0 + w] = val
        return
    for row in range(tm // GRID_W):
        for kb in range(N_COL_BLOCKS):
            src = row * GRID_W + kb * COL_BLOCK
            ref[0, kb, row * COL_BLOCK:(row + 1) * COL_BLOCK, c0:c0 + w] = val[src:src + COL_BLOCK]


def _qkv_kernel(x_ref, mod_ref, gain_ref, w_ref, gq_ref, gk_ref,
                q_ref, k_ref, v_ref, *, keys_by_col_block):
    m = mod_ref[...]
    shift = m[0:1]
    gain_scale = gain_ref[...] * (1.0 + m[1:2])
    h = _norm_mod(x_ref[0], gain_scale, shift).astype(BF16)
    ones_bd = _head_mean_matrix()
    gq = gq_ref[...] * (HEAD_DIM ** -0.5)
    gk = gk_ref[...]
    for j in range(D_MODEL // QKV_CHUNK):
        lo = j * QKV_CHUNK
        for base, g, out, permute in ((0, gq, q_ref, False),
                                      (D_MODEL, gk, k_ref, keys_by_col_block)):
            t2 = _dot(h, w_ref[:, base + lo:base + lo + QKV_CHUNK])
            for s in range(QKV_CHUNK // GROUP_W):
                t = t2[:, s * GROUP_W:(s + 1) * GROUP_W]
                ms = _dot((t * t).astype(BF16), ones_bd) * (1.0 / HEAD_DIM)
                normed = (t * lax.rsqrt(ms + EPS) * g).astype(BF16)
                _store_keys(out, normed, lo + s * GROUP_W, permute)
        v = _dot(h, w_ref[:, 2 * D_MODEL + lo:2 * D_MODEL + lo + QKV_CHUNK]).astype(BF16)
        _store_keys(v_ref, v, lo, keys_by_col_block)


def _qkv(x, mod, gain, w_qkv, gq, gk, *, tm, layer, mixer, cond_row, keys_by_col_block):
    bsz, n, _ = x.shape
    tok = pl.BlockSpec((1, tm, D_MODEL), lambda b, i: (b, i, 0))
    out = jax.ShapeDtypeStruct(x.shape, BF16)
    if keys_by_col_block:
        assert tm % GRID_W == 0
        kv_spec = pl.BlockSpec((1, N_COL_BLOCKS, tm // N_COL_BLOCKS, D_MODEL),
                               lambda b, i: (b, 0, i, 0))
        kv_out = jax.ShapeDtypeStruct((bsz, N_COL_BLOCKS, n // N_COL_BLOCKS, D_MODEL), BF16)
    else:
        kv_spec, kv_out = tok, out
    return pl.pallas_call(
        functools.partial(_qkv_kernel, keys_by_col_block=keys_by_col_block),
        grid=(bsz, n // tm),
        in_specs=[
            tok,
            _mod_spec(layer, cond_row),
            _layer_spec((1, D_MODEL), layer),
            _layer_spec((D_MODEL, 3 * D_MODEL), mixer),
            _layer_spec((1, GROUP_W), mixer),
            _layer_spec((1, GROUP_W), mixer),
        ],
        out_specs=[tok, kv_spec, kv_spec],
        out_shape=[out, kv_out, kv_out],
        compiler_params=_params("parallel", "arbitrary"),
        name="qkv",
    )(x, mod, gain, w_qkv, gq, gk)


def _stack_heads(q):
    lane_head = lax.broadcasted_iota(jnp.int32, q.shape, 1) // HEAD_DIM
    zero = jnp.zeros_like(q)
    return jnp.concatenate(
        [jnp.where(lane_head == hh, q, zero) for hh in range(HEADS_PER_GROUP)], axis=0)


def _unstack_heads(o, n):
    lane_head = lax.broadcasted_iota(jnp.int32, (n, GROUP_W), 1) // HEAD_DIM
    out = o[0:n]
    for hh in range(1, HEADS_PER_GROUP):
        out = jnp.where(lane_head == hh, o[hh * n:(hh + 1) * n], out)
    return out


def _window_scores(s):
    pieces = []
    for hh in range(HEADS_PER_GROUP):
        for q0, q1, kb0 in QUERY_SEGMENTS:
            pieces.append(s[hh * GRID_W + q0:hh * GRID_W + q1,
                            kb0 * WIN_BLOCK:kb0 * WIN_BLOCK + SEG_KEYS])
    return jnp.concatenate(pieces, axis=0)


def _window_probs(p):
    pieces = []
    for hh in range(HEADS_PER_GROUP):
        for q0, q1, kb0 in QUERY_SEGMENTS:
            n = q1 - q0
            parts = [p[hh * GRID_W + q0:hh * GRID_W + q1]]
            if kb0 > 0:
                parts.insert(0, jnp.zeros((n, kb0 * WIN_BLOCK), p.dtype))
            n_after = N_COL_BLOCKS - 2 - kb0
            if n_after > 0:
                parts.append(jnp.zeros((n, n_after * WIN_BLOCK), p.dtype))
            pieces.append(jnp.concatenate(parts, axis=1))
    return jnp.concatenate(pieces, axis=0)


def _nbr_attn_kernel(q_ref, k_ref, v_ref, kc_ref, vc_ref, bias_ref, o_ref):
    rows = pl.num_programs(1) * ROWS_PER_STEP
    r0 = pl.program_id(1) * ROWS_PER_STEP
    win0 = jnp.clip(r0 - NA_ROWS // 2, 0, rows - KEY_ROWS_PER_STEP)

    for i in range(ROWS_PER_STEP):
        r = r0 + i
        r_start = jnp.clip(r - NA_ROWS // 2, 0, rows - NA_ROWS)
        cls = r - r_start
        koff = pl.multiple_of((r_start - win0) * COL_BLOCK, COL_BLOCK)
        qoff = i * GRID_W
        for g in range(N_GROUPS):
            lo, hi = g * GROUP_W, (g + 1) * GROUP_W
            q = _stack_heads(q_ref[0, pl.ds(qoff, GRID_W), lo:hi])
            k_win = jnp.concatenate(
                [k_ref[0, kb, pl.ds(koff, WIN_BLOCK), lo:hi] for kb in range(N_COL_BLOCKS)],
                axis=0)
            v_win = jnp.concatenate(
                [v_ref[0, kb, pl.ds(koff, WIN_BLOCK), lo:hi] for kb in range(N_COL_BLOCKS)],
                axis=0)
            s_win = _window_scores(_dot_nt(q, k_win)) + bias_ref[cls, g]
            s_ctx = _dot_nt(q, kc_ref[0, :, lo:hi])
            mx = jnp.maximum(jnp.max(s_win, axis=-1, keepdims=True),
                             jnp.max(s_ctx, axis=-1, keepdims=True))
            p_win = jnp.exp(s_win - mx)
            p_ctx = jnp.exp(s_ctx - mx)
            denom = (jnp.sum(p_win, axis=-1, keepdims=True)
                     + jnp.sum(p_ctx, axis=-1, keepdims=True))
            pv = (_dot(_window_probs(p_win).astype(BF16), v_win)
                  + _dot(p_ctx.astype(BF16), vc_ref[0, :, lo:hi]))
            o = _unstack_heads(pv / denom, GRID_W)
            o_ref[0, pl.ds(qoff, GRID_W), lo:hi] = o.astype(BF16)


def _nbr_attn(q, k, v, kc, vc, bias):
    bsz, n, _ = q.shape
    rows = n // GRID_W
    tq = ROWS_PER_STEP * GRID_W
    n_ctx = kc.shape[1]
    assert k.shape == (bsz, N_COL_BLOCKS, rows * COL_BLOCK, D_MODEL)

    def key_map(b, i):
        start = jnp.clip(i * ROWS_PER_STEP - NA_ROWS // 2, 0, rows - KEY_ROWS_PER_STEP)
        return (b, 0, start * COL_BLOCK, 0)

    key_spec = pl.BlockSpec(
        (pl.Element(1), pl.Element(N_COL_BLOCKS), pl.Element(KEY_ROWS_PER_STEP * COL_BLOCK),
         pl.Element(D_MODEL)), key_map)
    ctx_spec = pl.BlockSpec((1, n_ctx, D_MODEL), lambda b, i: (b, 0, 0))
    return pl.pallas_call(
        _nbr_attn_kernel,
        grid=(bsz, rows // ROWS_PER_STEP),
        in_specs=[
            pl.BlockSpec((1, tq, D_MODEL), lambda b, i: (b, i, 0)),
            key_spec, key_spec, ctx_spec, ctx_spec,
            _const_spec(bias.shape),
        ],
        out_specs=pl.BlockSpec((1, tq, D_MODEL), lambda b, i: (b, i, 0)),
        out_shape=jax.ShapeDtypeStruct(q.shape, BF16),
        compiler_params=_params("parallel", "arbitrary"),
        name="nbr_attn",
    )(q, k, v, kc, vc, bias)


def _ctx_attn_kernel(q_ref, k_ref, v_ref, o_ref):
    n = q_ref.shape[1]
    q = _stack_heads(q_ref[0])
    s = _dot_nt(q, k_ref[0])
    p = jnp.exp(s - jnp.max(s, axis=-1, keepdims=True))
    denom = jnp.sum(p, axis=-1, keepdims=True)
    pv = _dot(p.astype(BF16), v_ref[0])
    o_ref[0] = _unstack_heads(pv / denom, n).astype(BF16)


def _ctx_attn(q, k, v):
    bsz, n, _ = q.shape
    spec = pl.BlockSpec((1, n, GROUP_W), lambda b, g: (b, 0, g))
    return pl.pallas_call(
        _ctx_attn_kernel,
        grid=(bsz, N_GROUPS),
        in_specs=[spec, spec, spec],
        out_specs=spec,
        out_shape=jax.ShapeDtypeStruct(q.shape, BF16),
        compiler_params=_params("parallel", "arbitrary"),
        name="ctx_attn",
    )(q, k, v)


def _bias_table(rpb):
    col = jnp.arange(GRID_W)
    col_start = jnp.clip(col - NA_COLS // 2, 0, GRID_W - NA_COLS)
    col_ok = (col[None, :] >= col_start[:, None]) & (col[None, :] < col_start[:, None] + NA_COLS)
    dc = jnp.clip(col[None, :] - col[:, None] + NA_COLS - 1, 0, 2 * NA_COLS - 2)
    onehot = (dc[None] == jnp.arange(2 * NA_COLS - 1)[:, None, None]).astype(F32)
    b15 = jnp.einsum("hrc,cqk->hrqk", rpb.astype(F32), onehot,
                     precision=lax.Precision.HIGHEST)
    b15 = jnp.where(col_ok[None, None], b15, NEG_INF)
    t = jnp.stack([b15[:, NA_ROWS - 1 - cls:2 * NA_ROWS - 1 - cls]
                   for cls in range(NA_ROWS)])
    t = t.reshape(NA_ROWS, N_HEADS, NA_ROWS, GRID_W, N_COL_BLOCKS, COL_BLOCK)
    t = t.transpose(0, 1, 3, 4, 2, 5)
    t = t.reshape(NA_ROWS, N_HEADS, GRID_W, N_COL_BLOCKS, WIN_BLOCK)
    t = jnp.concatenate(
        [t[:, :, q0:q1, kb0:kb0 + 2].reshape(NA_ROWS, N_HEADS, q1 - q0, SEG_KEYS)
         for q0, q1, kb0 in QUERY_SEGMENTS], axis=2)
    return t.reshape(NA_ROWS, N_GROUPS, HEADS_PER_GROUP * GRID_W, SEG_KEYS)


LATENT_TILE = 512
CTX_TILE = 512
COND_ROWS = 16


def kernel(x, c, ctx, c_ctx, w_ada, b_ada, norm_mix, norm_ffn, conv_w_in, conv_w,
           conv_w_out, attn_w_qkv, attn_q_norm, attn_k_norm, attn_rpb, attn_w_out,
           ffn_w_in, ffn_w_out):
    bsz, n, d = x.shape
    n_ctx = ctx.shape[1]
    assert d == D_MODEL and n % (ROWS_PER_STEP * GRID_W) == 0 and n % LATENT_TILE == 0
    assert n // GRID_W >= KEY_ROWS_PER_STEP and n_ctx % HALO == 0
    assert (bsz * n_ctx) % CTX_TILE == 0 and bsz < COND_ROWS
    flat = (1, bsz * n_ctx, d)
    per_batch = (bsz, n_ctx, d)
    ctx_row = bsz

    cond = jnp.concatenate(
        [c, c_ctx[None], jnp.zeros((COND_ROWS - bsz - 1, d), F32)], axis=0)
    mod = _ada(cond, w_ada, b_ada).reshape(DEPTH, COND_ROWS, 6, d)

    g_mix = norm_mix.reshape(DEPTH, 1, d)
    g_ffn = norm_ffn.reshape(DEPTH, 1, d)
    f_in = ffn_w_in.astype(BF16)
    f_out = ffn_w_out.astype(BF16)
    c_in = conv_w_in.astype(BF16)
    c_out = conv_w_out.astype(BF16)
    a_qkv = attn_w_qkv.astype(BF16)
    a_out = attn_w_out.astype(BF16)
    gq = jnp.tile(attn_q_norm, (1, HEADS_PER_GROUP))[:, None]
    gk = jnp.tile(attn_k_norm, (1, HEADS_PER_GROUP))[:, None]

    for i in range(DEPTH):
        update_ctx = i < DEPTH - 1
        j = i // 2
        lat = dict(layer=i, cond_row=None)
        con = dict(layer=i, cond_row=ctx_row)
        if i % 2 == 0:
            x = _conv_mixer(x, mod, g_mix, c_in, conv_w, c_out, tm=LATENT_TILE, mixer=j, **lat)
            x = _ffn(x, mod, g_ffn, f_in, f_out, tm=LATENT_TILE, **lat)
            if update_ctx:
                ctx = _conv_mixer(ctx, mod, g_mix, c_in, conv_w, c_out, tm=n_ctx, mixer=j, **con)
                ctx = _ffn(ctx.reshape(flat), mod, g_ffn, f_in, f_out, tm=CTX_TILE,
                           **con).reshape(per_batch)
        else:
            q, k, v = _qkv(x, mod, g_mix, a_qkv, gq, gk, tm=LATENT_TILE, mixer=j,
                           keys_by_col_block=True, **lat)
            qc, kc, vc = (t.reshape(per_batch) for t in _qkv(
                ctx.reshape(flat), mod, g_mix, a_qkv, gq, gk, tm=CTX_TILE, mixer=j,
                keys_by_col_block=False, **con))
            o = _nbr_attn(q, k, v, kc, vc, _bias_table(attn_rpb[j]))
            x = _ffn(x, mod, g_ffn, f_in, f_out, tm=LATENT_TILE, attn_o=o, w_o=a_out,
                     mixer=j, **lat)
            if update_ctx:
                oc = _ctx_attn(qc, kc, vc)
                ctx = _ffn(ctx.reshape(flat), mod, g_ffn, f_in, f_out, tm=CTX_TILE,
                           attn_o=oc.reshape(flat), w_o=a_out, mixer=j, **con).reshape(per_batch)
    return x
```

```python
import functools

import jax
import jax.numpy as jnp
from jax import lax
from jax.experimental import pallas as pl
from jax.experimental.pallas import tpu as pltpu

D_MODEL = 1024
DEPTH = 4
GRID_W = 64
N_HEADS = 16
HEAD_DIM = D_MODEL // N_HEADS
NA_ROWS = 8
NA_COLS = 16
FFN_HIDDEN = 2816
EPS = 1e-6
NEG_INF = -1e30

F32 = jnp.float32
BF16 = jnp.bfloat16

VMEM_LIMIT_BYTES = 56 * 1024 * 1024
HEADS_PER_GROUP = 4
GROUP_W = HEADS_PER_GROUP * HEAD_DIM
N_GROUPS = N_HEADS // HEADS_PER_GROUP
ROWS_PER_STEP = 8
KEY_ROWS_PER_STEP = ROWS_PER_STEP + NA_ROWS
COL_BLOCK = NA_COLS
N_COL_BLOCKS = GRID_W // COL_BLOCK
WIN_BLOCK = NA_ROWS * COL_BLOCK
QUERY_TILE = 8


def _query_segments():
    segs = []
    for q0 in range(0, GRID_W, QUERY_TILE):
        first = min(max(q0 - NA_COLS // 2, 0), GRID_W - NA_COLS)
        last = min(max(q0 + QUERY_TILE - 1 - NA_COLS // 2, 0), GRID_W - NA_COLS) + NA_COLS
        kb0 = min(first // COL_BLOCK, N_COL_BLOCKS - 2)
        assert last <= (kb0 + 2) * COL_BLOCK
        if segs and segs[-1][2] == kb0:
            segs[-1] = (segs[-1][0], q0 + QUERY_TILE, kb0)
        else:
            segs.append((q0, q0 + QUERY_TILE, kb0))
    return tuple(segs)


QUERY_SEGMENTS = _query_segments()
SEG_KEYS = 2 * WIN_BLOCK


def _params(*sem):
    return pltpu.CompilerParams(dimension_semantics=sem,
                                vmem_limit_bytes=VMEM_LIMIT_BYTES)


def _const_spec(shape):
    nd = len(shape)
    return pl.BlockSpec(shape, lambda *_: (0,) * nd, pipeline_mode=pl.Buffered(1))


def _layer_spec(shape, layer):
    nd = len(shape)
    return pl.BlockSpec((None,) + tuple(shape), lambda *_: (layer,) + (0,) * nd,
                        pipeline_mode=pl.Buffered(1))


def _mod_spec(layer, cond_row):
    if cond_row is None:
        return pl.BlockSpec((None, None, 6, D_MODEL), lambda b, i: (layer, b, 0, 0))
    return pl.BlockSpec((None, None, 6, D_MODEL), lambda b, i: (layer, cond_row, 0, 0))


def _silu(x):
    return x / (1.0 + jnp.exp(-x))


def _norm_mod(x, gain_scale, shift):
    ms = jnp.mean(x * x, axis=-1, keepdims=True)
    return x * lax.rsqrt(ms + EPS) * gain_scale + shift


def _dot(a, b):
    return jnp.dot(a, b, preferred_element_type=F32)


def _dot_nt(a, b):
    return lax.dot_general(a, b, (((1,), (1,)), ((), ())),
                           preferred_element_type=F32)


ADA_COLS = 1024


def _ada_kernel(c_ref, w_ref, b_ref, o_ref):
    a = _silu(c_ref[...]).astype(BF16)
    o_ref[0] = _dot(a, w_ref[0].astype(BF16)) + b_ref[0]


def _ada(cond, w_ada, b_ada):
    n_rows = cond.shape[0]
    n_cols = w_ada.shape[-1]
    return pl.pallas_call(
        _ada_kernel,
        grid=(DEPTH, n_cols // ADA_COLS),
        in_specs=[
            pl.BlockSpec((n_rows, D_MODEL), lambda l, j: (0, 0)),
            pl.BlockSpec((1, D_MODEL, ADA_COLS), lambda l, j: (l, 0, j)),
            pl.BlockSpec((1, 1, ADA_COLS), lambda l, j: (l, 0, j)),
        ],
        out_specs=pl.BlockSpec((1, n_rows, ADA_COLS), lambda l, j: (l, 0, j)),
        out_shape=jax.ShapeDtypeStruct((DEPTH, n_rows, n_cols), F32),
        compiler_params=_params("arbitrary", "arbitrary"),
        name="ada",
    )(cond, w_ada, b_ada.reshape(DEPTH, 1, n_cols))


CONV_CHUNK = 512
HALO = 8


def _conv_kernel(x_ref, xp_ref, xn_ref, mod_ref, gain_ref, win_ref, cw_ref,
                 wout_ref, o_ref, *, tm):
    i = pl.program_id(1)
    last = pl.num_programs(1) - 1
    m = mod_ref[...]
    shift = m[0:1]
    gain_scale = gain_ref[...] * (1.0 + m[1:2])
    gate = m[2:3]

    x = x_ref[0]
    x_ext = jnp.concatenate([x, xp_ref[0], xn_ref[0]], axis=0)
    h_ext = _norm_mod(x_ext, gain_scale, shift).astype(BF16)
    h = h_ext[:tm]
    has_prev = i > 0
    has_next = i < last

    cw = cw_ref[...]
    rows = lax.broadcasted_iota(jnp.int32, (tm, CONV_CHUNK), 0)
    acc = None
    for j in range(D_MODEL // CONV_CHUNK):
        lo, hi = j * CONV_CHUNK, (j + 1) * CONV_CHUNK
        b_gate = _dot(h, win_ref[:, lo:hi])
        c_gate = _dot(h_ext, win_ref[:, D_MODEL + lo:D_MODEL + hi])
        u = _dot(h_ext, win_ref[:, 2 * D_MODEL + lo:2 * D_MODEL + hi])
        z_ext = c_gate * u
        z = z_ext[:tm]
        z_prev = jnp.where(has_prev, z_ext[tm + HALO - 1:tm + HALO], 0.0)
        z_next = jnp.where(has_next, z_ext[tm + HALO:tm + HALO + 1], 0.0)
        z_up = jnp.where(rows == 0, z_prev, pltpu.roll(z, 1, 0))
        z_dn = jnp.where(rows == tm - 1, z_next, pltpu.roll(z, tm - 1, 0))
        zc = cw[0:1, lo:hi] * z_up + cw[1:2, lo:hi] * z + cw[2:3, lo:hi] * z_dn
        y = _dot((b_gate * zc).astype(BF16), wout_ref[lo:hi, :])
        acc = y if acc is None else acc + y
    o_ref[0] = x + gate * acc


def _conv_mixer(x, mod, gain, w_in, conv_w, w_out, *, tm, layer, mixer, cond_row):
    bsz, n, _ = x.shape
    nt = n // tm
    hb = tm // HALO
    n_hb = n // HALO
    return pl.pallas_call(
        functools.partial(_conv_kernel, tm=tm),
        grid=(bsz, nt),
        in_specs=[
            pl.BlockSpec((1, tm, D_MODEL), lambda b, i: (b, i, 0)),
            pl.BlockSpec((1, HALO, D_MODEL),
                         lambda b, i: (b, jnp.maximum(i * hb - 1, 0), 0)),
            pl.BlockSpec((1, HALO, D_MODEL),
                         lambda b, i: (b, jnp.minimum((i + 1) * hb, n_hb - 1), 0)),
            _mod_spec(layer, cond_row),
            _layer_spec((1, D_MODEL), layer),
            _layer_spec((D_MODEL, 3 * D_MODEL), mixer),
            _layer_spec((3, D_MODEL), mixer),
            _layer_spec((D_MODEL, D_MODEL), mixer),
        ],
        out_specs=pl.BlockSpec((1, tm, D_MODEL), lambda b, i: (b, i, 0)),
        out_shape=jax.ShapeDtypeStruct(x.shape, F32),
        compiler_params=_params("parallel", "arbitrary"),
        name="conv_mixer",
    )(x, x, x, mod, gain, w_in, conv_w, w_out)


FFN_CHUNK = 256


def _ffn_kernel(*refs, with_attn_out):
    if with_attn_out:
        x_ref, ao_ref, wo_ref, mod_ref, gain_ref, win_ref, wout_ref, o_ref = refs
    else:
        x_ref, mod_ref, gain_ref, win_ref, wout_ref, o_ref = refs
    m = mod_ref[...]
    shift = m[3:4]
    gain_scale = gain_ref[...] * (1.0 + m[4:5])
    gate = m[5:6]
    x = x_ref[0]
    if with_attn_out:
        x = x + m[2:3] * _dot(ao_ref[0], wo_ref[...])
    h = _norm_mod(x, gain_scale, shift).astype(BF16)
    acc = None
    for j in range(FFN_HIDDEN // FFN_CHUNK):
        lo, hi = j * FFN_CHUNK, (j + 1) * FFN_CHUNK
        g = _dot(h, win_ref[:, lo:hi])
        u = _dot(h, win_ref[:, FFN_HIDDEN + lo:FFN_HIDDEN + hi])
        y = _dot((_silu(g) * u).astype(BF16), wout_ref[lo:hi, :])
        acc = y if acc is None else acc + y
    o_ref[0] = x + gate * acc


def _ffn(x, mod, gain, w_in, w_out, *, tm, layer, cond_row, attn_o=None, w_o=None,
         mixer=None):
    bsz, n, _ = x.shape
    tok = pl.BlockSpec((1, tm, D_MODEL), lambda b, i: (b, i, 0))
    with_attn_out = attn_o is not None
    attn_args = (attn_o, w_o) if with_attn_out else ()
    attn_specs = [tok, _layer_spec((D_MODEL, D_MODEL), mixer)] if with_attn_out else []
    return pl.pallas_call(
        functools.partial(_ffn_kernel, with_attn_out=with_attn_out),
        grid=(bsz, n // tm),
        in_specs=[tok] + attn_specs + [
            _mod_spec(layer, cond_row),
            _layer_spec((1, D_MODEL), layer),
            _layer_spec((D_MODEL, 2 * FFN_HIDDEN), layer),
            _layer_spec((FFN_HIDDEN, D_MODEL), layer),
        ],
        out_specs=tok,
        out_shape=jax.ShapeDtypeStruct(x.shape, F32),
        compiler_params=_params("parallel", "arbitrary"),
        name="attn_out_ffn" if with_attn_out else "ffn",
    )(x, *attn_args, mod, gain, w_in, w_out)


QKV_CHUNK = 512


def _head_mean_matrix():
    r = lax.broadcasted_iota(jnp.int32, (GROUP_W, GROUP_W), 0) // HEAD_DIM
    c = lax.broadcasted_iota(jnp.int32, (GROUP_W, GROUP_W), 1) // HEAD_DIM
    return jnp.where(r == c, 1.0, 0.0).astype(BF16)


def _store_keys(ref, val, c0, by_col_block):
    tm, w = val.shape
    if not by_col_block:
        ref[0, :, c0:c0 + w] = val
        return
    for row in range(tm // GRID_W):
        for kb in range(N_COL_BLOCKS):
            src = row * GRID_W + kb * COL_BLOCK
            ref[0, kb, row * COL_BLOCK:(row + 1) * COL_BLOCK, c0:c0 + w] = val[src:src + COL_BLOCK]


def _qkv_kernel(x_ref, mod_ref, gain_ref, w_ref, gq_ref, gk_ref,
                q_ref, k_ref, v_ref, *, keys_by_col_block):
    m = mod_ref[...]
    shift = m[0:1]
    gain_scale = gain_ref[...] * (1.0 + m[1:2])
    h = _norm_mod(x_ref[0], gain_scale, shift).astype(BF16)
    ones_bd = _head_mean_matrix()
    gq = gq_ref[...] * (HEAD_DIM ** -0.5)
    gk = gk_ref[...]
    for j in range(D_MODEL // QKV_CHUNK):
        lo = j * QKV_CHUNK
        for base, g, out, permute in ((0, gq, q_ref, False),
                                      (D_MODEL, gk, k_ref, keys_by_col_block)):
            t2 = _dot(h, w_ref[:, base + lo:base + lo + QKV_CHUNK])
            for s in range(QKV_CHUNK // GROUP_W):
                t = t2[:, s * GROUP_W:(s + 1) * GROUP_W]
                ms = _dot((t * t).astype(BF16), ones_bd) * (1.0 / HEAD_DIM)
                normed = (t * lax.rsqrt(ms + EPS) * g).astype(BF16)
                _store_keys(out, normed, lo + s * GROUP_W, permute)
        v = _dot(h, w_ref[:, 2 * D_MODEL + lo:2 * D_MODEL + lo + QKV_CHUNK]).astype(BF16)
        _store_keys(v_ref, v, lo, keys_by_col_block)


def _qkv(x, mod, gain, w_qkv, gq, gk, *, tm, layer, mixer, cond_row, keys_by_col_block):
    bsz, n, _ = x.shape
    tok = pl.BlockSpec((1, tm, D_MODEL), lambda b, i: (b, i, 0))
    out = jax.ShapeDtypeStruct(x.shape, BF16)
    if keys_by_col_block:
        assert tm % GRID_W == 0
        kv_spec = pl.BlockSpec((1, N_COL_BLOCKS, tm // N_COL_BLOCKS, D_MODEL),
                               lambda b, i: (b, 0, i, 0))
        kv_out = jax.ShapeDtypeStruct((bsz, N_COL_BLOCKS, n // N_COL_BLOCKS, D_MODEL), BF16)
    else:
        kv_spec, kv_out = tok, out
    return pl.pallas_call(
        functools.partial(_qkv_kernel, keys_by_col_block=keys_by_col_block),
        grid=(bsz, n // tm),
        in_specs=[
            tok,
            _mod_spec(layer, cond_row),
            _layer_spec((1, D_MODEL), layer),
            _layer_spec((D_MODEL, 3 * D_MODEL), mixer),
            _layer_spec((1, GROUP_W), mixer),
            _layer_spec((1, GROUP_W), mixer),
        ],
        out_specs=[tok, kv_spec, kv_spec],
        out_shape=[out, kv_out, kv_out],
        compiler_params=_params("parallel", "arbitrary"),
        name="qkv",
    )(x, mod, gain, w_qkv, gq, gk)


def _stack_heads(q):
    lane_head = lax.broadcasted_iota(jnp.int32, q.shape, 1) // HEAD_DIM
    zero = jnp.zeros_like(q)
    return jnp.concatenate(
        [jnp.where(lane_head == hh, q, zero) for hh in range(HEADS_PER_GROUP)], axis=0)


def _unstack_heads(o, n):
    lane_head = lax.broadcasted_iota(jnp.int32, (n, GROUP_W), 1) // HEAD_DIM
    out = o[0:n]
    for hh in range(1, HEADS_PER_GROUP):
        out = jnp.where(lane_head == hh, o[hh * n:(hh + 1) * n], out)
    return out


def _window_scores(s):
    pieces = []
    for hh in range(HEADS_PER_GROUP):
        for q0, q1, kb0 in QUERY_SEGMENTS:
            pieces.append(s[hh * GRID_W + q0:hh * GRID_W + q1,
                            kb0 * WIN_BLOCK:kb0 * WIN_BLOCK + SEG_KEYS])
    return jnp.concatenate(pieces, axis=0)


def _window_probs(p):
    pieces = []
    for hh in range(HEADS_PER_GROUP):
        for q0, q1, kb0 in QUERY_SEGMENTS:
            n = q1 - q0
            parts = [p[hh * GRID_W + q0:hh * GRID_W + q1]]
            if kb0 > 0:
                parts.insert(0, jnp.zeros((n, kb0 * WIN_BLOCK), p.dtype))
            n_after = N_COL_BLOCKS - 2 - kb0
            if n_after > 0:
                parts.append(jnp.zeros((n, n_after * WIN_BLOCK), p.dtype))
            pieces.append(jnp.concatenate(parts, axis=1))
    return jnp.concatenate(pieces, axis=0)


def _nbr_attn_kernel(q_ref, k_ref, v_ref, kc_ref, vc_ref, bias_ref, o_ref):
    rows = pl.num_programs(1) * ROWS_PER_STEP
    r0 = pl.program_id(1) * ROWS_PER_STEP
    win0 = jnp.clip(r0 - NA_ROWS // 2, 0, rows - KEY_ROWS_PER_STEP)

    for i in range(ROWS_PER_STEP):
        r = r0 + i
        r_start = jnp.clip(r - NA_ROWS // 2, 0, rows - NA_ROWS)
        cls = r - r_start
        koff = pl.multiple_of((r_start - win0) * COL_BLOCK, COL_BLOCK)
        qoff = i * GRID_W
        for g in range(N_GROUPS):
            lo, hi = g * GROUP_W, (g + 1) * GROUP_W
            q = _stack_heads(q_ref[0, pl.ds(qoff, GRID_W), lo:hi])
            k_win = jnp.concatenate(
                [k_ref[0, kb, pl.ds(koff, WIN_BLOCK), lo:hi] for kb in range(N_COL_BLOCKS)],
                axis=0)
            v_win = jnp.concatenate(
                [v_ref[0, kb, pl.ds(koff, WIN_BLOCK), lo:hi] for kb in range(N_COL_BLOCKS)],
                axis=0)
            s_win = _window_scores(_dot_nt(q, k_win)) + bias_ref[cls, g]
            s_ctx = _dot_nt(q, kc_ref[0, :, lo:hi])
            s = jnp.concatenate([s_win, s_ctx], axis=1)
            p = jnp.exp(s - jnp.max(s, axis=-1, keepdims=True))
            denom = jnp.sum(p, axis=-1, keepdims=True)
            pv = (_dot(_window_probs(p[:, :SEG_KEYS]).astype(BF16), v_win)
                  + _dot(p[:, SEG_KEYS:].astype(BF16), vc_ref[0, :, lo:hi]))
            o = _unstack_heads(pv / denom, GRID_W)
            o_ref[0, pl.ds(qoff, GRID_W), lo:hi] = o.astype(BF16)


def _nbr_attn(q, k, v, kc, vc, bias):
    bsz, n, _ = q.shape
    rows = n // GRID_W
    tq = ROWS_PER_STEP * GRID_W
    n_ctx = kc.shape[1]
    assert k.shape == (bsz, N_COL_BLOCKS, rows * COL_BLOCK, D_MODEL)

    def key_map(b, i):
        start = jnp.clip(i * ROWS_PER_STEP - NA_ROWS // 2, 0, rows - KEY_ROWS_PER_STEP)
        return (b, 0, start * COL_BLOCK, 0)

    key_spec = pl.BlockSpec(
        (pl.Element(1), pl.Element(N_COL_BLOCKS), pl.Element(KEY_ROWS_PER_STEP * COL_BLOCK),
         pl.Element(D_MODEL)), key_map)
    ctx_spec = pl.BlockSpec((1, n_ctx, D_MODEL), lambda b, i: (b, 0, 0))
    return pl.pallas_call(
        _nbr_attn_kernel,
        grid=(bsz, rows // ROWS_PER_STEP),
        in_specs=[
            pl.BlockSpec((1, tq, D_MODEL), lambda b, i: (b, i, 0)),
            key_spec, key_spec, ctx_spec, ctx_spec,
            _const_spec(bias.shape),
        ],
        out_specs=pl.BlockSpec((1, tq, D_MODEL), lambda b, i: (b, i, 0)),
        out_shape=jax.ShapeDtypeStruct(q.shape, BF16),
        compiler_params=_params("parallel", "arbitrary"),
        name="nbr_attn",
    )(q, k, v, kc, vc, bias)


def _ctx_attn_kernel(q_ref, k_ref, v_ref, o_ref):
    n = q_ref.shape[1]
    q = _stack_heads(q_ref[0])
    s = _dot_nt(q, k_ref[0])
    p = jnp.exp(s - jnp.max(s, axis=-1, keepdims=True))
    denom = jnp.sum(p, axis=-1, keepdims=True)
    pv = _dot(p.astype(BF16), v_ref[0])
    o_ref[0] = _unstack_heads(pv / denom, n).astype(BF16)


def _ctx_attn(q, k, v):
    bsz, n, _ = q.shape
    spec = pl.BlockSpec((1, n, GROUP_W), lambda b, g: (b, 0, g))
    return pl.pallas_call(
        _ctx_attn_kernel,
        grid=(bsz, N_GROUPS),
        in_specs=[spec, spec, spec],
        out_specs=spec,
        out_shape=jax.ShapeDtypeStruct(q.shape, BF16),
        compiler_params=_params("parallel", "arbitrary"),
        name="ctx_attn",
    )(q, k, v)


def _bias_table(rpb):
    q = jnp.arange(GRID_W)[:, None]
    lane = jnp.arange(SEG_KEYS)[None, :]
    kb0 = jnp.concatenate([jnp.full((q1 - q0,), s_kb0) for q0, q1, s_kb0 in QUERY_SEGMENTS])
    k = (kb0[:, None] + lane // WIN_BLOCK) * COL_BLOCK + lane % COL_BLOCK
    j_of_lane = (lane[0] % WIN_BLOCK) // COL_BLOCK
    k_start = jnp.clip(q - NA_COLS // 2, 0, GRID_W - NA_COLS)
    col_ok = (k >= k_start) & (k < k_start + NA_COLS)
    dc = jnp.clip(k - q + NA_COLS - 1, 0, 2 * NA_COLS - 2)
    onehot = (dc[None] == jnp.arange(2 * NA_COLS - 1)[:, None, None]).astype(F32)
    by_dr = jnp.einsum("hrc,cql->hrql", rpb.astype(F32), onehot,
                       precision=lax.Precision.HIGHEST)
    row_sel = (j_of_lane[None] == jnp.arange(NA_ROWS)[:, None]).astype(F32)
    t = jnp.stack([
        sum(by_dr[:, j - cls + NA_ROWS - 1] * row_sel[j] for j in range(NA_ROWS))
        for cls in range(NA_ROWS)])
    t = jnp.where(col_ok[None, None], t, NEG_INF)
    return t.reshape(NA_ROWS, N_GROUPS, HEADS_PER_GROUP * GRID_W, SEG_KEYS)


LATENT_TILE = 1024
CTX_TILE = 512
COND_ROWS = 16


def kernel(x, c, ctx, c_ctx, w_ada, b_ada, norm_mix, norm_ffn, conv_w_in, conv_w,
           conv_w_out, attn_w_qkv, attn_q_norm, attn_k_norm, attn_rpb, attn_w_out,
           ffn_w_in, ffn_w_out):
    bsz, n, d = x.shape
    n_ctx = ctx.shape[1]
    assert d == D_MODEL and n % (ROWS_PER_STEP * GRID_W) == 0 and n % LATENT_TILE == 0
    assert n // GRID_W >= KEY_ROWS_PER_STEP and n_ctx % HALO == 0
    assert (bsz * n_ctx) % CTX_TILE == 0 and bsz < COND_ROWS
    flat = (1, bsz * n_ctx, d)
    per_batch = (bsz, n_ctx, d)
    ctx_row = bsz

    cond = jnp.concatenate(
        [c, c_ctx[None], jnp.zeros((COND_ROWS - bsz - 1, d), F32)], axis=0)
    mod = _ada(cond, w_ada, b_ada).reshape(DEPTH, COND_ROWS, 6, d)

    g_mix = norm_mix.reshape(DEPTH, 1, d)
    g_ffn = norm_ffn.reshape(DEPTH, 1, d)
    f_in = ffn_w_in.astype(BF16)
    f_out = ffn_w_out.astype(BF16)
    c_in = conv_w_in.astype(BF16)
    c_out = conv_w_out.astype(BF16)
    a_qkv = attn_w_qkv.astype(BF16)
    a_out = attn_w_out.astype(BF16)
    gq = jnp.tile(attn_q_norm, (1, HEADS_PER_GROUP))[:, None]
    gk = jnp.tile(attn_k_norm, (1, HEADS_PER_GROUP))[:, None]

    for i in range(DEPTH):
        update_ctx = i < DEPTH - 1
        j = i // 2
        lat = dict(layer=i, cond_row=None)
        con = dict(layer=i, cond_row=ctx_row)
        if i % 2 == 0:
            x = _conv_mixer(x, mod, g_mix, c_in, conv_w, c_out, tm=LATENT_TILE, mixer=j, **lat)
            x = _ffn(x, mod, g_ffn, f_in, f_out, tm=LATENT_TILE, **lat)
            if update_ctx:
                ctx = _conv_mixer(ctx, mod, g_mix, c_in, conv_w, c_out, tm=n_ctx, mixer=j, **con)
                ctx = _ffn(ctx.reshape(flat), mod, g_ffn, f_in, f_out, tm=CTX_TILE,
                           **con).reshape(per_batch)
        else:
            q, k, v = _qkv(x, mod, g_mix, a_qkv, gq, gk, tm=LATENT_TILE, mixer=j,
                           keys_by_col_block=True, **lat)
            qc, kc, vc = (t.reshape(per_batch) for t in _qkv(
                ctx.reshape(flat), mod, g_mix, a_qkv, gq, gk, tm=CTX_TILE, mixer=j,
                keys_by_col_block=False, **con))
            o = _nbr_attn(q, k, v, kc, vc, _bias_table(attn_rpb[j]))
            x = _ffn(x, mod, g_ffn, f_in, f_out, tm=LATENT_TILE, attn_o=o, w_o=a_out,
                     mixer=j, **lat)
            if update_ctx:
                oc = _ctx_attn(qc, kc, vc)
                ctx = _ffn(ctx.reshape(flat), mod, g_ffn, f_in, f_out, tm=CTX_TILE,
                           attn_o=oc.reshape(flat), w_o=a_out, mixer=j, **con).reshape(per_batch)
    return x
```

```python
import functools

import jax
import jax.numpy as jnp
from jax import lax
from jax.experimental import pallas as pl
from jax.experimental.pallas import tpu as pltpu

D_MODEL = 1024
DEPTH = 4
GRID_W = 64
N_HEADS = 16
HEAD_DIM = D_MODEL // N_HEADS
NA_ROWS = 8
NA_COLS = 16
FFN_HIDDEN = 2816
EPS = 1e-6
NEG_INF = -1e30

F32 = jnp.float32
BF16 = jnp.bfloat16

VMEM_LIMIT_BYTES = 56 * 1024 * 1024
HEADS_PER_GROUP = 4
GROUP_W = HEADS_PER_GROUP * HEAD_DIM
N_GROUPS = N_HEADS // HEADS_PER_GROUP
ROWS_PER_STEP = 8
KEY_ROWS_PER_STEP = ROWS_PER_STEP + NA_ROWS
COL_BLOCK = NA_COLS
N_COL_BLOCKS = GRID_W // COL_BLOCK
WIN_BLOCK = NA_ROWS * COL_BLOCK
QUERY_TILE = 8


def _query_segments():
    segs = []
    for q0 in range(0, GRID_W, QUERY_TILE):
        first = min(max(q0 - NA_COLS // 2, 0), GRID_W - NA_COLS)
        last = min(max(q0 + QUERY_TILE - 1 - NA_COLS // 2, 0), GRID_W - NA_COLS) + NA_COLS
        kb0 = min(first // COL_BLOCK, N_COL_BLOCKS - 2)
        assert last <= (kb0 + 2) * COL_BLOCK
        if segs and segs[-1][2] == kb0:
            segs[-1] = (segs[-1][0], q0 + QUERY_TILE, kb0)
        else:
            segs.append((q0, q0 + QUERY_TILE, kb0))
    return tuple(segs)


QUERY_SEGMENTS = _query_segments()
SEG_KEYS = 2 * WIN_BLOCK


def _params(*sem):
    return pltpu.CompilerParams(dimension_semantics=sem,
                                vmem_limit_bytes=VMEM_LIMIT_BYTES)


def _const_spec(shape):
    nd = len(shape)
    return pl.BlockSpec(shape, lambda *_: (0,) * nd, pipeline_mode=pl.Buffered(1))


def _layer_spec(shape, layer):
    nd = len(shape)
    return pl.BlockSpec((None,) + tuple(shape), lambda *_: (layer,) + (0,) * nd,
                        pipeline_mode=pl.Buffered(1))


def _mod_spec(layer, cond_row):
    if cond_row is None:
        return pl.BlockSpec((None, None, 6, D_MODEL), lambda b, i: (layer, b, 0, 0))
    return pl.BlockSpec((None, None, 6, D_MODEL), lambda b, i: (layer, cond_row, 0, 0))


def _silu(x):
    return x / (1.0 + jnp.exp(-x))


def _norm_mod(x, gain_scale, shift):
    ms = jnp.mean(x * x, axis=-1, keepdims=True)
    return x * lax.rsqrt(ms + EPS) * gain_scale + shift


def _dot(a, b):
    return jnp.dot(a, b, preferred_element_type=F32)


def _dot_nt(a, b):
    return lax.dot_general(a, b, (((1,), (1,)), ((), ())),
                           preferred_element_type=F32)


ADA_COLS = 1024


def _ada_kernel(c_ref, w_ref, b_ref, o_ref):
    a = _silu(c_ref[...]).astype(BF16)
    o_ref[0] = _dot(a, w_ref[0].astype(BF16)) + b_ref[0]


def _ada(cond, w_ada, b_ada):
    n_rows = cond.shape[0]
    n_cols = w_ada.shape[-1]
    return pl.pallas_call(
        _ada_kernel,
        grid=(DEPTH, n_cols // ADA_COLS),
        in_specs=[
            pl.BlockSpec((n_rows, D_MODEL), lambda l, j: (0, 0)),
            pl.BlockSpec((1, D_MODEL, ADA_COLS), lambda l, j: (l, 0, j)),
            pl.BlockSpec((1, 1, ADA_COLS), lambda l, j: (l, 0, j)),
        ],
        out_specs=pl.BlockSpec((1, n_rows, ADA_COLS), lambda l, j: (l, 0, j)),
        out_shape=jax.ShapeDtypeStruct((DEPTH, n_rows, n_cols), F32),
        compiler_params=_params("arbitrary", "arbitrary"),
        name="ada",
    )(cond, w_ada, b_ada.reshape(DEPTH, 1, n_cols))


CONV_CHUNK = 512
HALO = 8


def _conv_kernel(x_ref, xp_ref, xn_ref, mod_ref, gain_ref, win_ref, cw_ref,
                 wout_ref, o_ref, *, tm):
    i = pl.program_id(1)
    last = pl.num_programs(1) - 1
    m = mod_ref[...]
    shift = m[0:1]
    gain_scale = gain_ref[...] * (1.0 + m[1:2])
    gate = m[2:3]

    x = x_ref[0]
    x_ext = jnp.concatenate([x, xp_ref[0], xn_ref[0]], axis=0)
    h_ext = _norm_mod(x_ext, gain_scale, shift).astype(BF16)
    h = h_ext[:tm]
    has_prev = i > 0
    has_next = i < last

    cw = cw_ref[...]
    rows = lax.broadcasted_iota(jnp.int32, (tm, CONV_CHUNK), 0)
    acc = None
    for j in range(D_MODEL // CONV_CHUNK):
        lo, hi = j * CONV_CHUNK, (j + 1) * CONV_CHUNK
        b_gate = _dot(h, win_ref[:, lo:hi])
        c_gate = _dot(h_ext, win_ref[:, D_MODEL + lo:D_MODEL + hi])
        u = _dot(h_ext, win_ref[:, 2 * D_MODEL + lo:2 * D_MODEL + hi])
        z_ext = c_gate * u
        z = z_ext[:tm]
        z_prev = jnp.where(has_prev, z_ext[tm + HALO - 1:tm + HALO], 0.0)
        z_next = jnp.where(has_next, z_ext[tm + HALO:tm + HALO + 1], 0.0)
        z_up = jnp.where(rows == 0, z_prev, pltpu.roll(z, 1, 0))
        z_dn = jnp.where(rows == tm - 1, z_next, pltpu.roll(z, tm - 1, 0))
        zc = cw[0:1, lo:hi] * z_up + cw[1:2, lo:hi] * z + cw[2:3, lo:hi] * z_dn
        y = _dot((b_gate * zc).astype(BF16), wout_ref[lo:hi, :])
        acc = y if acc is None else acc + y
    o_ref[0] = x + gate * acc


def _conv_mixer(x, mod, gain, w_in, conv_w, w_out, *, tm, layer, mixer, cond_row):
    bsz, n, _ = x.shape
    nt = n // tm
    hb = tm // HALO
    n_hb = n // HALO
    return pl.pallas_call(
        functools.partial(_conv_kernel, tm=tm),
        grid=(bsz, nt),
        in_specs=[
            pl.BlockSpec((1, tm, D_MODEL), lambda b, i: (b, i, 0)),
            pl.BlockSpec((1, HALO, D_MODEL),
                         lambda b, i: (b, jnp.maximum(i * hb - 1, 0), 0)),
            pl.BlockSpec((1, HALO, D_MODEL),
                         lambda b, i: (b, jnp.minimum((i + 1) * hb, n_hb - 1), 0)),
            _mod_spec(layer, cond_row),
            _layer_spec((1, D_MODEL), layer),
            _layer_spec((D_MODEL, 3 * D_MODEL), 0),
            _layer_spec((3, D_MODEL), mixer),
            _layer_spec((D_MODEL, D_MODEL), 0),
        ],
        out_specs=pl.BlockSpec((1, tm, D_MODEL), lambda b, i: (b, i, 0)),
        out_shape=jax.ShapeDtypeStruct(x.shape, F32),
        compiler_params=_params("parallel", "arbitrary"),
        name="conv_mixer",
    )(x, x, x, mod, gain, w_in, conv_w, w_out)


FFN_CHUNK = 256


def _ffn_kernel(*refs, with_attn_out, n_cast):
    n_in = len(refs) - 1 - 2 * n_cast
    cast_in = refs[n_in:n_in + n_cast]
    o_ref = refs[n_in + n_cast]
    cast_out = refs[n_in + n_cast + 1:]
    if with_attn_out:
        x_ref, ao_ref, wo_ref, mod_ref, gain_ref, win_ref, wout_ref = refs[:n_in]
    else:
        x_ref, mod_ref, gain_ref, win_ref, wout_ref = refs[:n_in]
    for src, dst in zip(cast_in, cast_out):
        dst[...] = src[...].astype(BF16)
    m = mod_ref[...]
    shift = m[3:4]
    gain_scale = gain_ref[...] * (1.0 + m[4:5])
    gate = m[5:6]
    x = x_ref[0]
    if with_attn_out:
        x = x + m[2:3] * _dot(ao_ref[0], wo_ref[...])
    h = _norm_mod(x, gain_scale, shift).astype(BF16)
    acc = None
    for lo in range(0, FFN_HIDDEN, FFN_CHUNK):
        hi = min(lo + FFN_CHUNK, FFN_HIDDEN)
        g = _dot(h, win_ref[:, lo:hi])
        u = _dot(h, win_ref[:, FFN_HIDDEN + lo:FFN_HIDDEN + hi])
        y = _dot((_silu(g) * u).astype(BF16), wout_ref[lo:hi, :])
        acc = y if acc is None else acc + y
    o_ref[0] = x + gate * acc


BF16_ROW_TILE = 16
LANE_TILE = 128


def _cast_blocking(r, c, steps):
    for cb in range(1, steps + 1):
        rb = steps // cb
        if (steps % cb == 0 and r % rb == 0 and c % cb == 0
                and (r // rb) % BF16_ROW_TILE == 0 and (c // cb) % LANE_TILE == 0):
            return rb, cb
    raise ValueError(f"no tile-aligned split of a ({r}, {c}) parameter over {steps} grid steps")


def _ffn(x, mod, gain, w_in, w_out, *, tm, layer, cond_row, attn_o=None, w_o=None, cast=()):
    bsz, n, _ = x.shape
    nt = n // tm
    steps = bsz * nt
    tok = pl.BlockSpec((1, tm, D_MODEL), lambda b, i: (b, i, 0))
    with_attn_out = attn_o is not None
    attn_args = (attn_o, w_o) if with_attn_out else ()
    attn_specs = [tok, _layer_spec((D_MODEL, D_MODEL), 0)] if with_attn_out else []
    cast_in_specs, cast_out_specs, cast_shapes = [], [], []
    for w, w_layer in cast:
        _, r, c = w.shape
        rb, cb = _cast_blocking(r, c, steps)
        cast_in_specs.append(pl.BlockSpec(
            (None, r // rb, c // cb),
            lambda b, i, w_layer=w_layer, cb=cb: (w_layer, (b * nt + i) // cb, (b * nt + i) % cb)))
        cast_out_specs.append(pl.BlockSpec(
            (None, r // rb, c // cb),
            lambda b, i, cb=cb: (0, (b * nt + i) // cb, (b * nt + i) % cb)))
        cast_shapes.append(jax.ShapeDtypeStruct((1, r, c), BF16))
    outs = pl.pallas_call(
        functools.partial(_ffn_kernel, with_attn_out=with_attn_out, n_cast=len(cast)),
        grid=(bsz, nt),
        in_specs=[tok] + attn_specs + [
            _mod_spec(layer, cond_row),
            _layer_spec((1, D_MODEL), layer),
            _layer_spec((D_MODEL, 2 * FFN_HIDDEN), 0),
            _layer_spec((FFN_HIDDEN, D_MODEL), 0),
        ] + cast_in_specs,
        out_specs=[tok] + cast_out_specs,
        out_shape=[jax.ShapeDtypeStruct(x.shape, F32)] + cast_shapes,
        compiler_params=_params("parallel", "arbitrary"),
        name="attn_out_ffn" if with_attn_out else "ffn",
    )(x, *attn_args, mod, gain, w_in, w_out, *[w for w, _ in cast])
    return outs[0], list(outs[1:])


QKV_CHUNK = 512


def _head_mean_matrix():
    r = lax.broadcasted_iota(jnp.int32, (GROUP_W, GROUP_W), 0) // HEAD_DIM
    c = lax.broadcasted_iota(jnp.int32, (GROUP_W, GROUP_W), 1) // HEAD_DIM
    return jnp.where(r == c, 1.0, 0.0).astype(BF16)


def _store_keys(ref, val, c0, by_col_block):
    tm, w = val.shape
    if not by_col_block:
        ref[0, :, c0:c0 + w] = val
        return
    for row in range(tm // GRID_W):
        for kb in range(N_COL_BLOCKS):
            src = row * GRID_W + kb * COL_BLOCK
            ref[0, kb, row * COL_BLOCK:(row + 1) * COL_BLOCK, c0:c0 + w] = val[src:src + COL_BLOCK]


def _qkv_kernel(x_ref, mod_ref, gain_ref, w_ref, gq_ref, gk_ref,
                q_ref, k_ref, v_ref, *, keys_by_col_block):
    m = mod_ref[...]
    shift = m[0:1]
    gain_scale = gain_ref[...] * (1.0 + m[1:2])
    h = _norm_mod(x_ref[0], gain_scale, shift).astype(BF16)
    ones_bd = _head_mean_matrix()
    gq = gq_ref[...] * (HEAD_DIM ** -0.5)
    gk = gk_ref[...]
    for j in range(D_MODEL // QKV_CHUNK):
        lo = j * QKV_CHUNK
        for base, g, out, permute in ((0, gq, q_ref, False),
                                      (D_MODEL, gk, k_ref, keys_by_col_block)):
            t2 = _dot(h, w_ref[:, base + lo:base + lo + QKV_CHUNK])
            for s in range(QKV_CHUNK // GROUP_W):
                t = t2[:, s * GROUP_W:(s + 1) * GROUP_W]
                ms = _dot((t * t).astype(BF16), ones_bd) * (1.0 / HEAD_DIM)
                normed = (t * lax.rsqrt(ms + EPS) * g).astype(BF16)
                _store_keys(out, normed, lo + s * GROUP_W, permute)
        v = _dot(h, w_ref[:, 2 * D_MODEL + lo:2 * D_MODEL + lo + QKV_CHUNK]).astype(BF16)
        _store_keys(v_ref, v, lo, keys_by_col_block)


def _qkv(x, mod, gain, w_qkv, gq, gk, *, tm, layer, mixer, cond_row, keys_by_col_block):
    bsz, n, _ = x.shape
    tok = pl.BlockSpec((1, tm, D_MODEL), lambda b, i: (b, i, 0))
    out = jax.ShapeDtypeStruct(x.shape, BF16)
    if keys_by_col_block:
        assert tm % GRID_W == 0
        kv_spec = pl.BlockSpec((1, N_COL_BLOCKS, tm // N_COL_BLOCKS, D_MODEL),
                               lambda b, i: (b, 0, i, 0))
        kv_out = jax.ShapeDtypeStruct((bsz, N_COL_BLOCKS, n // N_COL_BLOCKS, D_MODEL), BF16)
    else:
        kv_spec, kv_out = tok, out
    return pl.pallas_call(
        functools.partial(_qkv_kernel, keys_by_col_block=keys_by_col_block),
        grid=(bsz, n // tm),
        in_specs=[
            tok,
            _mod_spec(layer, cond_row),
            _layer_spec((1, D_MODEL), layer),
            _layer_spec((D_MODEL, 3 * D_MODEL), 0),
            _layer_spec((1, GROUP_W), mixer),
            _layer_spec((1, GROUP_W), mixer),
        ],
        out_specs=[tok, kv_spec, kv_spec],
        out_shape=[out, kv_out, kv_out],
        compiler_params=_params("parallel", "arbitrary"),
        name="qkv",
    )(x, mod, gain, w_qkv, gq, gk)


def _stack_heads(q):
    lane_head = lax.broadcasted_iota(jnp.int32, q.shape, 1) // HEAD_DIM
    zero = jnp.zeros_like(q)
    return jnp.concatenate(
        [jnp.where(lane_head == hh, q, zero) for hh in range(HEADS_PER_GROUP)], axis=0)


def _unstack_heads(o, n):
    lane_head = lax.broadcasted_iota(jnp.int32, (n, GROUP_W), 1) // HEAD_DIM
    out = o[0:n]
    for hh in range(1, HEADS_PER_GROUP):
        out = jnp.where(lane_head == hh, o[hh * n:(hh + 1) * n], out)
    return out


def _window_scores(s):
    pieces = []
    for hh in range(HEADS_PER_GROUP):
        for q0, q1, kb0 in QUERY_SEGMENTS:
            pieces.append(s[hh * GRID_W + q0:hh * GRID_W + q1,
                            kb0 * WIN_BLOCK:kb0 * WIN_BLOCK + SEG_KEYS])
    return jnp.concatenate(pieces, axis=0)


def _window_probs(p):
    pieces = []
    for hh in range(HEADS_PER_GROUP):
        for q0, q1, kb0 in QUERY_SEGMENTS:
            n = q1 - q0
            parts = [p[hh * GRID_W + q0:hh * GRID_W + q1]]
            if kb0 > 0:
                parts.insert(0, jnp.zeros((n, kb0 * WIN_BLOCK), p.dtype))
            n_after = N_COL_BLOCKS - 2 - kb0
            if n_after > 0:
                parts.append(jnp.zeros((n, n_after * WIN_BLOCK), p.dtype))
            pieces.append(jnp.concatenate(parts, axis=1))
    return jnp.concatenate(pieces, axis=0)


def _nbr_attn_kernel(q_ref, k_ref, v_ref, kc_ref, vc_ref, bias_ref, o_ref):
    rows = pl.num_programs(1) * ROWS_PER_STEP
    r0 = pl.program_id(1) * ROWS_PER_STEP
    win0 = jnp.clip(r0 - NA_ROWS // 2, 0, rows - KEY_ROWS_PER_STEP)
    n_win = N_COL_BLOCKS * WIN_BLOCK

    for i in range(ROWS_PER_STEP):
        r = r0 + i
        r_start = jnp.clip(r - NA_ROWS // 2, 0, rows - NA_ROWS)
        cls = r - r_start
        koff = pl.multiple_of((r_start - win0) * COL_BLOCK, COL_BLOCK)
        qoff = i * GRID_W
        for g in range(N_GROUPS):
            lo, hi = g * GROUP_W, (g + 1) * GROUP_W
            q = _stack_heads(q_ref[0, pl.ds(qoff, GRID_W), lo:hi])
            k_all = jnp.concatenate(
                [k_ref[0, kb, pl.ds(koff, WIN_BLOCK), lo:hi] for kb in range(N_COL_BLOCKS)]
                + [kc_ref[0, :, lo:hi]], axis=0)
            v_all = jnp.concatenate(
                [v_ref[0, kb, pl.ds(koff, WIN_BLOCK), lo:hi] for kb in range(N_COL_BLOCKS)]
                + [vc_ref[0, :, lo:hi]], axis=0)
            s_all = _dot_nt(q, k_all)
            s_win = _window_scores(s_all[:, :n_win]) + bias_ref[cls, g]
            s = jnp.concatenate([s_win, s_all[:, n_win:]], axis=1)
            p = jnp.exp(s - jnp.max(s, axis=-1, keepdims=True))
            denom = jnp.sum(p, axis=-1, keepdims=True)
            p_all = jnp.concatenate([_window_probs(p[:, :SEG_KEYS]), p[:, SEG_KEYS:]], axis=1)
            pv = _dot(p_all.astype(BF16), v_all)
            o = _unstack_heads(pv / denom, GRID_W)
            o_ref[0, pl.ds(qoff, GRID_W), lo:hi] = o.astype(BF16)


def _nbr_attn(q, k, v, kc, vc, bias):
    bsz, n, _ = q.shape
    rows = n // GRID_W
    tq = ROWS_PER_STEP * GRID_W
    n_ctx = kc.shape[1]
    assert k.shape == (bsz, N_COL_BLOCKS, rows * COL_BLOCK, D_MODEL)

    def key_map(b, i):
        start = jnp.clip(i * ROWS_PER_STEP - NA_ROWS // 2, 0, rows - KEY_ROWS_PER_STEP)
        return (b, 0, start * COL_BLOCK, 0)

    key_spec = pl.BlockSpec(
        (pl.Element(1), pl.Element(N_COL_BLOCKS), pl.Element(KEY_ROWS_PER_STEP * COL_BLOCK),
         pl.Element(D_MODEL)), key_map)
    ctx_spec = pl.BlockSpec((1, n_ctx, D_MODEL), lambda b, i: (b, 0, 0))
    return pl.pallas_call(
        _nbr_attn_kernel,
        grid=(bsz, rows // ROWS_PER_STEP),
        in_specs=[
            pl.BlockSpec((1, tq, D_MODEL), lambda b, i: (b, i, 0)),
            key_spec, key_spec, ctx_spec, ctx_spec,
            _const_spec(bias.shape),
        ],
        out_specs=pl.BlockSpec((1, tq, D_MODEL), lambda b, i: (b, i, 0)),
        out_shape=jax.ShapeDtypeStruct(q.shape, BF16),
        compiler_params=_params("parallel", "arbitrary"),
        name="nbr_attn",
    )(q, k, v, kc, vc, bias)


def _ctx_attn_kernel(q_ref, k_ref, v_ref, o_ref):
    n = q_ref.shape[1]
    q = _stack_heads(q_ref[0])
    s = _dot_nt(q, k_ref[0])
    p = jnp.exp(s - jnp.max(s, axis=-1, keepdims=True))
    denom = jnp.sum(p, axis=-1, keepdims=True)
    pv = _dot(p.astype(BF16), v_ref[0])
    o_ref[0] = _unstack_heads(pv / denom, n).astype(BF16)


def _ctx_attn(q, k, v):
    bsz, n, _ = q.shape
    spec = pl.BlockSpec((1, n, GROUP_W), lambda b, g: (b, 0, g))
    return pl.pallas_call(
        _ctx_attn_kernel,
        grid=(bsz, N_GROUPS),
        in_specs=[spec, spec, spec],
        out_specs=spec,
        out_shape=jax.ShapeDtypeStruct(q.shape, BF16),
        compiler_params=_params("parallel", "arbitrary"),
        name="ctx_attn",
    )(q, k, v)


def _bias_table(rpb):
    q = jnp.arange(GRID_W)[:, None]
    lane = jnp.arange(SEG_KEYS)[None, :]
    kb0 = jnp.concatenate([jnp.full((q1 - q0,), s_kb0) for q0, q1, s_kb0 in QUERY_SEGMENTS])
    k = (kb0[:, None] + lane // WIN_BLOCK) * COL_BLOCK + lane % COL_BLOCK
    j_of_lane = (lane[0] % WIN_BLOCK) // COL_BLOCK
    k_start = jnp.clip(q - NA_COLS // 2, 0, GRID_W - NA_COLS)
    col_ok = (k >= k_start) & (k < k_start + NA_COLS)
    dc = jnp.clip(k - q + NA_COLS - 1, 0, 2 * NA_COLS - 2)
    onehot = (dc[None] == jnp.arange(2 * NA_COLS - 1)[:, None, None]).astype(F32)
    by_dr = jnp.einsum("hrc,cql->hrql", rpb.astype(F32), onehot,
                       precision=lax.Precision.HIGHEST)
    row_sel = (j_of_lane[None] == jnp.arange(NA_ROWS)[:, None]).astype(F32)
    t = jnp.stack([
        sum(by_dr[:, j - cls + NA_ROWS - 1] * row_sel[j] for j in range(NA_ROWS))
        for cls in range(NA_ROWS)])
    t = jnp.where(col_ok[None, None], t, NEG_INF)
    return t.reshape(NA_ROWS, N_GROUPS, HEADS_PER_GROUP * GRID_W, SEG_KEYS)


LATENT_TILE = 1024
CTX_TILE = 512
COND_ROWS = 16


def kernel(x, c, ctx, c_ctx, w_ada, b_ada, norm_mix, norm_ffn, conv_w_in, conv_w,
           conv_w_out, attn_w_qkv, attn_q_norm, attn_k_norm, attn_rpb, attn_w_out,
           ffn_w_in, ffn_w_out):
    bsz, n, d = x.shape
    n_ctx = ctx.shape[1]
    assert d == D_MODEL and n % (ROWS_PER_STEP * GRID_W) == 0 and n % LATENT_TILE == 0
    assert n // GRID_W >= KEY_ROWS_PER_STEP and n_ctx % HALO == 0
    assert (bsz * n_ctx) % CTX_TILE == 0 and bsz < COND_ROWS
    flat = (1, bsz * n_ctx, d)
    per_batch = (bsz, n_ctx, d)
    ctx_row = bsz

    cond = jnp.concatenate(
        [c, c_ctx[None], jnp.zeros((COND_ROWS - bsz - 1, d), F32)], axis=0)
    mod = _ada(cond, w_ada, b_ada).reshape(DEPTH, COND_ROWS, 6, d)

    g_mix = norm_mix.reshape(DEPTH, 1, d)
    g_ffn = norm_ffn.reshape(DEPTH, 1, d)
    gq = jnp.tile(attn_q_norm, (1, HEADS_PER_GROUP))[:, None]
    gk = jnp.tile(attn_k_norm, (1, HEADS_PER_GROUP))[:, None]

    def matmul_params(i):
        mix_in, mix_out = (conv_w_in, conv_w_out) if i % 2 == 0 else (attn_w_qkv, attn_w_out)
        return [(mix_in, i // 2), (mix_out, i // 2), (ffn_w_in, i), (ffn_w_out, i)]

    m_in, m_out, f_in, f_out = [w[l:l + 1].astype(BF16) for w, l in matmul_params(0)]

    for i in range(DEPTH):
        update_ctx = i < DEPTH - 1
        j = i // 2
        lat = dict(layer=i, cond_row=None)
        con = dict(layer=i, cond_row=ctx_row)
        cast = matmul_params(i + 1) if i + 1 < DEPTH else ()
        if i % 2 == 0:
            x = _conv_mixer(x, mod, g_mix, m_in, conv_w, m_out, tm=LATENT_TILE, mixer=j, **lat)
            x, nxt = _ffn(x, mod, g_ffn, f_in, f_out, tm=LATENT_TILE, cast=cast, **lat)
            if update_ctx:
                ctx = _conv_mixer(ctx, mod, g_mix, m_in, conv_w, m_out, tm=n_ctx, mixer=j, **con)
                ctx = _ffn(ctx.reshape(flat), mod, g_ffn, f_in, f_out, tm=CTX_TILE,
                           **con)[0].reshape(per_batch)
        else:
            q, k, v = _qkv(x, mod, g_mix, m_in, gq, gk, tm=LATENT_TILE, mixer=j,
                           keys_by_col_block=True, **lat)
            qc, kc, vc = (t.reshape(per_batch) for t in _qkv(
                ctx.reshape(flat), mod, g_mix, m_in, gq, gk, tm=CTX_TILE, mixer=j,
                keys_by_col_block=False, **con))
            o = _nbr_attn(q, k, v, kc, vc, _bias_table(attn_rpb[j]))
            x, nxt = _ffn(x, mod, g_ffn, f_in, f_out, tm=LATENT_TILE, attn_o=o, w_o=m_out,
                          cast=cast, **lat)
            if update_ctx:
                oc = _ctx_attn(qc, kc, vc)
                ctx = _ffn(ctx.reshape(flat), mod, g_ffn, f_in, f_out, tm=CTX_TILE,
                           attn_o=oc.reshape(flat), w_o=m_out, **con)[0].reshape(per_batch)
        if nxt:
            m_in, m_out, f_in, f_out = nxt
    return x
```

```python
import functools

import jax
import jax.numpy as jnp
from jax import lax
from jax.experimental import pallas as pl
from jax.experimental.pallas import tpu as pltpu

D_MODEL = 1024
DEPTH = 4
GRID_W = 64
N_HEADS = 16
HEAD_DIM = D_MODEL // N_HEADS
NA_ROWS = 8
NA_COLS = 16
FFN_HIDDEN = 2816
EPS = 1e-6
NEG_INF = -1e30

F32 = jnp.float32
BF16 = jnp.bfloat16

VMEM_LIMIT_BYTES = 56 * 1024 * 1024
HEADS_PER_GROUP = 4
GROUP_W = HEADS_PER_GROUP * HEAD_DIM
N_GROUPS = N_HEADS // HEADS_PER_GROUP
ROWS_PER_STEP = 8
KEY_ROWS_PER_STEP = ROWS_PER_STEP + NA_ROWS
COL_BLOCK = NA_COLS
N_COL_BLOCKS = GRID_W // COL_BLOCK
WIN_BLOCK = NA_ROWS * COL_BLOCK
QUERY_TILE = 8


def _query_segments():
    segs = []
    for q0 in range(0, GRID_W, QUERY_TILE):
        first = min(max(q0 - NA_COLS // 2, 0), GRID_W - NA_COLS)
        last = min(max(q0 + QUERY_TILE - 1 - NA_COLS // 2, 0), GRID_W - NA_COLS) + NA_COLS
        kb0 = min(first // COL_BLOCK, N_COL_BLOCKS - 2)
        assert last <= (kb0 + 2) * COL_BLOCK
        if segs and segs[-1][2] == kb0:
            segs[-1] = (segs[-1][0], q0 + QUERY_TILE, kb0)
        else:
            segs.append((q0, q0 + QUERY_TILE, kb0))
    return tuple(segs)


QUERY_SEGMENTS = _query_segments()
SEG_KEYS = 2 * WIN_BLOCK


def _params(*sem):
    return pltpu.CompilerParams(dimension_semantics=sem,
                                vmem_limit_bytes=VMEM_LIMIT_BYTES)


def _const_spec(shape):
    nd = len(shape)
    return pl.BlockSpec(shape, lambda *_: (0,) * nd, pipeline_mode=pl.Buffered(1))


def _layer_spec(shape, layer):
    nd = len(shape)
    return pl.BlockSpec((None,) + tuple(shape), lambda *_: (layer,) + (0,) * nd,
                        pipeline_mode=pl.Buffered(1))


def _mod_spec(layer, cond_row):
    if cond_row is None:
        return pl.BlockSpec((None, None, 6, D_MODEL), lambda b, i: (layer, b, 0, 0))
    return pl.BlockSpec((None, None, 6, D_MODEL), lambda b, i: (layer, cond_row, 0, 0))


def _silu(x):
    return x / (1.0 + jnp.exp(-x))


def _norm_mod(x, gain_scale, shift):
    ms = jnp.mean(x * x, axis=-1, keepdims=True)
    return x * lax.rsqrt(ms + EPS) * gain_scale + shift


def _dot(a, b):
    return jnp.dot(a, b, preferred_element_type=F32)


def _dot_nt(a, b):
    return lax.dot_general(a, b, (((1,), (1,)), ((), ())),
                           preferred_element_type=F32)


ADA_COLS = 1024


def _ada_kernel(c_ref, w_ref, b_ref, o_ref):
    a = _silu(c_ref[...]).astype(BF16)
    o_ref[0] = _dot(a, w_ref[0].astype(BF16)) + b_ref[0]


def _ada(cond, w_ada, b_ada):
    n_rows = cond.shape[0]
    n_cols = w_ada.shape[-1]
    return pl.pallas_call(
        _ada_kernel,
        grid=(DEPTH, n_cols // ADA_COLS),
        in_specs=[
            pl.BlockSpec((n_rows, D_MODEL), lambda l, j: (0, 0)),
            pl.BlockSpec((1, D_MODEL, ADA_COLS), lambda l, j: (l, 0, j)),
            pl.BlockSpec((1, 1, ADA_COLS), lambda l, j: (l, 0, j)),
        ],
        out_specs=pl.BlockSpec((1, n_rows, ADA_COLS), lambda l, j: (l, 0, j)),
        out_shape=jax.ShapeDtypeStruct((DEPTH, n_rows, n_cols), F32),
        compiler_params=_params("arbitrary", "arbitrary"),
        name="ada",
    )(cond, w_ada, b_ada.reshape(DEPTH, 1, n_cols))


CONV_CHUNK = 512
HALO = 8


def _conv_kernel(x_ref, xp_ref, xn_ref, mod_ref, gain_ref, win_ref, cw_ref,
                 wout_ref, o_ref, *, tm):
    i = pl.program_id(1)
    last = pl.num_programs(1) - 1
    m = mod_ref[...]
    shift = m[0:1]
    gain_scale = gain_ref[...] * (1.0 + m[1:2])
    gate = m[2:3]

    x = x_ref[0]
    x_ext = jnp.concatenate([x, xp_ref[0], xn_ref[0]], axis=0)
    h_ext = _norm_mod(x_ext, gain_scale, shift).astype(BF16)
    h = h_ext[:tm]
    has_prev = i > 0
    has_next = i < last

    cw = cw_ref[...]
    rows = lax.broadcasted_iota(jnp.int32, (tm, CONV_CHUNK), 0)
    acc = None
    for j in range(D_MODEL // CONV_CHUNK):
        lo, hi = j * CONV_CHUNK, (j + 1) * CONV_CHUNK
        b_gate = _dot(h, win_ref[:, lo:hi])
        c_gate = _dot(h_ext, win_ref[:, D_MODEL + lo:D_MODEL + hi])
        u = _dot(h_ext, win_ref[:, 2 * D_MODEL + lo:2 * D_MODEL + hi])
        z_ext = c_gate * u
        z = z_ext[:tm]
        z_prev = jnp.where(has_prev, z_ext[tm + HALO - 1:tm + HALO], 0.0)
        z_next = jnp.where(has_next, z_ext[tm + HALO:tm + HALO + 1], 0.0)
        z_up = jnp.where(rows == 0, z_prev, pltpu.roll(z, 1, 0))
        z_dn = jnp.where(rows == tm - 1, z_next, pltpu.roll(z, tm - 1, 0))
        zc = cw[0:1, lo:hi] * z_up + cw[1:2, lo:hi] * z + cw[2:3, lo:hi] * z_dn
        y = _dot((b_gate * zc).astype(BF16), wout_ref[lo:hi, :])
        acc = y if acc is None else acc + y
    o_ref[0] = x + gate * acc


def _conv_mixer(x, mod, gain, w_in, conv_w, w_out, *, tm, layer, mixer, cond_row):
    bsz, n, _ = x.shape
    nt = n // tm
    hb = tm // HALO
    n_hb = n // HALO
    return pl.pallas_call(
        functools.partial(_conv_kernel, tm=tm),
        grid=(bsz, nt),
        in_specs=[
            pl.BlockSpec((1, tm, D_MODEL), lambda b, i: (b, i, 0)),
            pl.BlockSpec((1, HALO, D_MODEL),
                         lambda b, i: (b, jnp.maximum(i * hb - 1, 0), 0)),
            pl.BlockSpec((1, HALO, D_MODEL),
                         lambda b, i: (b, jnp.minimum((i + 1) * hb, n_hb - 1), 0)),
            _mod_spec(layer, cond_row),
            _layer_spec((1, D_MODEL), layer),
            _layer_spec((D_MODEL, 3 * D_MODEL), 0),
            _layer_spec((3, D_MODEL), mixer),
            _layer_spec((D_MODEL, D_MODEL), 0),
        ],
        out_specs=pl.BlockSpec((1, tm, D_MODEL), lambda b, i: (b, i, 0)),
        out_shape=jax.ShapeDtypeStruct(x.shape, F32),
        compiler_params=_params("parallel", "arbitrary"),
        name="conv_mixer",
    )(x, x, x, mod, gain, w_in, conv_w, w_out)


FFN_CHUNK = 256


def _ffn_kernel(*refs, with_attn_out, n_cast):
    n_in = len(refs) - 1 - 2 * n_cast
    cast_in = refs[n_in:n_in + n_cast]
    o_ref = refs[n_in + n_cast]
    cast_out = refs[n_in + n_cast + 1:]
    if with_attn_out:
        x_ref, ao_ref, wo_ref, mod_ref, gain_ref, win_ref, wout_ref = refs[:n_in]
    else:
        x_ref, mod_ref, gain_ref, win_ref, wout_ref = refs[:n_in]
    for src, dst in zip(cast_in, cast_out):
        dst[...] = src[...].astype(BF16)
    m = mod_ref[...]
    shift = m[3:4]
    gain_scale = gain_ref[...] * (1.0 + m[4:5])
    gate = m[5:6]
    x = x_ref[0]
    if with_attn_out:
        x = x + m[2:3] * _dot(ao_ref[0], wo_ref[...])
    h = _norm_mod(x, gain_scale, shift).astype(BF16)
    acc = None
    for lo in range(0, FFN_HIDDEN, FFN_CHUNK):
        hi = min(lo + FFN_CHUNK, FFN_HIDDEN)
        g = _dot(h, win_ref[:, lo:hi])
        u = _dot(h, win_ref[:, FFN_HIDDEN + lo:FFN_HIDDEN + hi])
        y = _dot((_silu(g) * u).astype(BF16), wout_ref[lo:hi, :])
        acc = y if acc is None else acc + y
    o_ref[0] = x + gate * acc


BF16_ROW_TILE = 16
LANE_TILE = 128


def _cast_blocking(r, c, steps):
    for cb in range(1, steps + 1):
        rb = steps // cb
        if (steps % cb == 0 and r % rb == 0 and c % cb == 0
                and (r // rb) % BF16_ROW_TILE == 0 and (c // cb) % LANE_TILE == 0):
            return rb, cb
    raise ValueError(f"no tile-aligned split of a ({r}, {c}) parameter over {steps} grid steps")


CAST_STEPS = 32


def _cast_kernel(*refs):
    n = len(refs) // 2
    for src, dst in zip(refs[:n], refs[n:]):
        dst[...] = src[...].astype(BF16)


def _cast_params(params):
    in_specs, out_specs, shapes = [], [], []
    for w, w_layer in params:
        _, r, c = w.shape
        rb, cb = _cast_blocking(r, c, CAST_STEPS)
        blk = (None, r // rb, c // cb)
        in_specs.append(pl.BlockSpec(
            blk, lambda s, w_layer=w_layer, cb=cb: (w_layer, s // cb, s % cb)))
        out_specs.append(pl.BlockSpec(blk, lambda s, cb=cb: (0, s // cb, s % cb)))
        shapes.append(jax.ShapeDtypeStruct((1, r, c), BF16))
    return pl.pallas_call(
        _cast_kernel,
        grid=(CAST_STEPS,),
        in_specs=in_specs,
        out_specs=out_specs,
        out_shape=shapes,
        compiler_params=_params("arbitrary"),
        name="cast_params",
    )(*[w for w, _ in params])


def _ffn(x, mod, gain, w_in, w_out, *, tm, layer, cond_row, attn_o=None, w_o=None, cast=()):
    bsz, n, _ = x.shape
    nt = n // tm
    steps = bsz * nt
    tok = pl.BlockSpec((1, tm, D_MODEL), lambda b, i: (b, i, 0))
    with_attn_out = attn_o is not None
    attn_args = (attn_o, w_o) if with_attn_out else ()
    attn_specs = [tok, _layer_spec((D_MODEL, D_MODEL), 0)] if with_attn_out else []
    cast_in_specs, cast_out_specs, cast_shapes = [], [], []
    for w, w_layer in cast:
        _, r, c = w.shape
        rb, cb = _cast_blocking(r, c, steps)
        cast_in_specs.append(pl.BlockSpec(
            (None, r // rb, c // cb),
            lambda b, i, w_layer=w_layer, cb=cb: (w_layer, (b * nt + i) // cb, (b * nt + i) % cb)))
        cast_out_specs.append(pl.BlockSpec(
            (None, r // rb, c // cb),
            lambda b, i, cb=cb: (0, (b * nt + i) // cb, (b * nt + i) % cb)))
        cast_shapes.append(jax.ShapeDtypeStruct((1, r, c), BF16))
    outs = pl.pallas_call(
        functools.partial(_ffn_kernel, with_attn_out=with_attn_out, n_cast=len(cast)),
        grid=(bsz, nt),
        in_specs=[tok] + attn_specs + [
            _mod_spec(layer, cond_row),
            _layer_spec((1, D_MODEL), layer),
            _layer_spec((D_MODEL, 2 * FFN_HIDDEN), 0),
            _layer_spec((FFN_HIDDEN, D_MODEL), 0),
        ] + cast_in_specs,
        out_specs=[tok] + cast_out_specs,
        out_shape=[jax.ShapeDtypeStruct(x.shape, F32)] + cast_shapes,
        compiler_params=_params("parallel", "arbitrary"),
        name="attn_out_ffn" if with_attn_out else "ffn",
    )(x, *attn_args, mod, gain, w_in, w_out, *[w for w, _ in cast])
    return outs[0], list(outs[1:])


QKV_CHUNK = 512


def _head_mean_matrix():
    r = lax.broadcasted_iota(jnp.int32, (GROUP_W, GROUP_W), 0) // HEAD_DIM
    c = lax.broadcasted_iota(jnp.int32, (GROUP_W, GROUP_W), 1) // HEAD_DIM
    return jnp.where(r == c, 1.0, 0.0).astype(BF16)


def _store_keys(ref, val, c0, by_col_block):
    tm, w = val.shape
    if not by_col_block:
        ref[0, :, c0:c0 + w] = val
        return
    for row in range(tm // GRID_W):
        for kb in range(N_COL_BLOCKS):
            src = row * GRID_W + kb * COL_BLOCK
            ref[0, kb, row * COL_BLOCK:(row + 1) * COL_BLOCK, c0:c0 + w] = val[src:src + COL_BLOCK]


def _qkv_kernel(x_ref, mod_ref, gain_ref, w_ref, gq_ref, gk_ref,
                q_ref, k_ref, v_ref, *, keys_by_col_block):
    m = mod_ref[...]
    shift = m[0:1]
    gain_scale = gain_ref[...] * (1.0 + m[1:2])
    h = _norm_mod(x_ref[0], gain_scale, shift).astype(BF16)
    ones_bd = _head_mean_matrix()
    gq = gq_ref[...] * (HEAD_DIM ** -0.5)
    gk = gk_ref[...]
    for j in range(D_MODEL // QKV_CHUNK):
        lo = j * QKV_CHUNK
        for base, g, out, permute in ((0, gq, q_ref, False),
                                      (D_MODEL, gk, k_ref, keys_by_col_block)):
            t2 = _dot(h, w_ref[:, base + lo:base + lo + QKV_CHUNK])
            for s in range(QKV_CHUNK // GROUP_W):
                t = t2[:, s * GROUP_W:(s + 1) * GROUP_W]
                ms = _dot((t * t).astype(BF16), ones_bd) * (1.0 / HEAD_DIM)
                normed = (t * lax.rsqrt(ms + EPS) * g).astype(BF16)
                _store_keys(out, normed, lo + s * GROUP_W, permute)
        v = _dot(h, w_ref[:, 2 * D_MODEL + lo:2 * D_MODEL + lo + QKV_CHUNK]).astype(BF16)
        _store_keys(v_ref, v, lo, keys_by_col_block)


def _qkv(x, mod, gain, w_qkv, gq, gk, *, tm, layer, mixer, cond_row, keys_by_col_block):
    bsz, n, _ = x.shape
    tok = pl.BlockSpec((1, tm, D_MODEL), lambda b, i: (b, i, 0))
    out = jax.ShapeDtypeStruct(x.shape, BF16)
    if keys_by_col_block:
        assert tm % GRID_W == 0
        kv_spec = pl.BlockSpec((1, N_COL_BLOCKS, tm // N_COL_BLOCKS, D_MODEL),
                               lambda b, i: (b, 0, i, 0))
        kv_out = jax.ShapeDtypeStruct((bsz, N_COL_BLOCKS, n // N_COL_BLOCKS, D_MODEL), BF16)
    else:
        kv_spec, kv_out = tok, out
    return pl.pallas_call(
        functools.partial(_qkv_kernel, keys_by_col_block=keys_by_col_block),
        grid=(bsz, n // tm),
        in_specs=[
            tok,
            _mod_spec(layer, cond_row),
            _layer_spec((1, D_MODEL), layer),
            _layer_spec((D_MODEL, 3 * D_MODEL), 0),
            _layer_spec((1, GROUP_W), mixer),
            _layer_spec((1, GROUP_W), mixer),
        ],
        out_specs=[tok, kv_spec, kv_spec],
        out_shape=[out, kv_out, kv_out],
        compiler_params=_params("parallel", "arbitrary"),
        name="qkv",
    )(x, mod, gain, w_qkv, gq, gk)


def _stack_heads(q):
    lane_head = lax.broadcasted_iota(jnp.int32, q.shape, 1) // HEAD_DIM
    zero = jnp.zeros_like(q)
    return jnp.concatenate(
        [jnp.where(lane_head == hh, q, zero) for hh in range(HEADS_PER_GROUP)], axis=0)


def _unstack_heads(o, n):
    lane_head = lax.broadcasted_iota(jnp.int32, (n, GROUP_W), 1) // HEAD_DIM
    out = o[0:n]
    for hh in range(1, HEADS_PER_GROUP):
        out = jnp.where(lane_head == hh, o[hh * n:(hh + 1) * n], out)
    return out


def _window_scores(s):
    pieces = []
    for hh in range(HEADS_PER_GROUP):
        for q0, q1, kb0 in QUERY_SEGMENTS:
            pieces.append(s[hh * GRID_W + q0:hh * GRID_W + q1,
                            kb0 * WIN_BLOCK:kb0 * WIN_BLOCK + SEG_KEYS])
    return jnp.concatenate(pieces, axis=0)


def _window_probs(p):
    pieces = []
    for hh in range(HEADS_PER_GROUP):
        for q0, q1, kb0 in QUERY_SEGMENTS:
            n = q1 - q0
            parts = [p[hh * GRID_W + q0:hh * GRID_W + q1]]
            if kb0 > 0:
                parts.insert(0, jnp.zeros((n, kb0 * WIN_BLOCK), p.dtype))
            n_after = N_COL_BLOCKS - 2 - kb0
            if n_after > 0:
                parts.append(jnp.zeros((n, n_after * WIN_BLOCK), p.dtype))
            pieces.append(jnp.concatenate(parts, axis=1))
    return jnp.concatenate(pieces, axis=0)


def _nbr_attn_kernel(q_ref, k_ref, v_ref, kc_ref, vc_ref, bias_ref, o_ref):
    rows = pl.num_programs(1) * ROWS_PER_STEP
    r0 = pl.program_id(1) * ROWS_PER_STEP
    win0 = jnp.clip(r0 - NA_ROWS // 2, 0, rows - KEY_ROWS_PER_STEP)
    n_win = N_COL_BLOCKS * WIN_BLOCK

    for i in range(ROWS_PER_STEP):
        r = r0 + i
        r_start = jnp.clip(r - NA_ROWS // 2, 0, rows - NA_ROWS)
        cls = r - r_start
        koff = pl.multiple_of((r_start - win0) * COL_BLOCK, COL_BLOCK)
        qoff = i * GRID_W
        for g in range(N_GROUPS):
            lo, hi = g * GROUP_W, (g + 1) * GROUP_W
            q = _stack_heads(q_ref[0, pl.ds(qoff, GRID_W), lo:hi])
            k_all = jnp.concatenate(
                [k_ref[0, kb, pl.ds(koff, WIN_BLOCK), lo:hi] for kb in range(N_COL_BLOCKS)]
                + [kc_ref[0, :, lo:hi]], axis=0)
            v_all = jnp.concatenate(
                [v_ref[0, kb, pl.ds(koff, WIN_BLOCK), lo:hi] for kb in range(N_COL_BLOCKS)]
                + [vc_ref[0, :, lo:hi]], axis=0)
            s_all = _dot_nt(q, k_all)
            s_win = _window_scores(s_all[:, :n_win]) + bias_ref[cls, g]
            s = jnp.concatenate([s_win, s_all[:, n_win:]], axis=1)
            p = jnp.exp(s - jnp.max(s, axis=-1, keepdims=True))
            denom = jnp.sum(p, axis=-1, keepdims=True)
            p_all = jnp.concatenate([_window_probs(p[:, :SEG_KEYS]), p[:, SEG_KEYS:]], axis=1)
            pv = _dot(p_all.astype(BF16), v_all)
            o = _unstack_heads(pv / denom, GRID_W)
            o_ref[0, pl.ds(qoff, GRID_W), lo:hi] = o.astype(BF16)


def _nbr_attn(q, k, v, kc, vc, bias):
    bsz, n, _ = q.shape
    rows = n // GRID_W
    tq = ROWS_PER_STEP * GRID_W
    n_ctx = kc.shape[1]
    assert k.shape == (bsz, N_COL_BLOCKS, rows * COL_BLOCK, D_MODEL)

    def key_map(b, i):
        start = jnp.clip(i * ROWS_PER_STEP - NA_ROWS // 2, 0, rows - KEY_ROWS_PER_STEP)
        return (b, 0, start * COL_BLOCK, 0)

    key_spec = pl.BlockSpec(
        (pl.Element(1), pl.Element(N_COL_BLOCKS), pl.Element(KEY_ROWS_PER_STEP * COL_BLOCK),
         pl.Element(D_MODEL)), key_map)
    ctx_spec = pl.BlockSpec((1, n_ctx, D_MODEL), lambda b, i: (b, 0, 0))
    return pl.pallas_call(
        _nbr_attn_kernel,
        grid=(bsz, rows // ROWS_PER_STEP),
        in_specs=[
            pl.BlockSpec((1, tq, D_MODEL), lambda b, i: (b, i, 0)),
            key_spec, key_spec, ctx_spec, ctx_spec,
            _const_spec(bias.shape),
        ],
        out_specs=pl.BlockSpec((1, tq, D_MODEL), lambda b, i: (b, i, 0)),
        out_shape=jax.ShapeDtypeStruct(q.shape, BF16),
        compiler_params=_params("parallel", "arbitrary"),
        name="nbr_attn",
    )(q, k, v, kc, vc, bias)


def _ctx_attn_kernel(q_ref, k_ref, v_ref, o_ref):
    n = q_ref.shape[1]
    q = _stack_heads(q_ref[0])
    s = _dot_nt(q, k_ref[0])
    p = jnp.exp(s - jnp.max(s, axis=-1, keepdims=True))
    denom = jnp.sum(p, axis=-1, keepdims=True)
    pv = _dot(p.astype(BF16), v_ref[0])
    o_ref[0] = _unstack_heads(pv / denom, n).astype(BF16)


def _ctx_attn(q, k, v):
    bsz, n, _ = q.shape
    spec = pl.BlockSpec((1, n, GROUP_W), lambda b, g: (b, 0, g))
    return pl.pallas_call(
        _ctx_attn_kernel,
        grid=(bsz, N_GROUPS),
        in_specs=[spec, spec, spec],
        out_specs=spec,
        out_shape=jax.ShapeDtypeStruct(q.shape, BF16),
        compiler_params=_params("parallel", "arbitrary"),
        name="ctx_attn",
    )(q, k, v)


def _bias_table(rpb):
    q = jnp.arange(GRID_W)[:, None]
    lane = jnp.arange(SEG_KEYS)[None, :]
    kb0 = jnp.concatenate([jnp.full((q1 - q0,), s_kb0) for q0, q1, s_kb0 in QUERY_SEGMENTS])
    k = (kb0[:, None] + lane // WIN_BLOCK) * COL_BLOCK + lane % COL_BLOCK
    j_of_lane = (lane[0] % WIN_BLOCK) // COL_BLOCK
    k_start = jnp.clip(q - NA_COLS // 2, 0, GRID_W - NA_COLS)
    col_ok = (k >= k_start) & (k < k_start + NA_COLS)
    dc = jnp.clip(k - q + NA_COLS - 1, 0, 2 * NA_COLS - 2)
    onehot = (dc[None] == jnp.arange(2 * NA_COLS - 1)[:, None, None]).astype(F32)
    by_dr = jnp.einsum("hrc,cql->hrql", rpb.astype(F32), onehot,
                       precision=lax.Precision.HIGHEST)
    row_sel = (j_of_lane[None] == jnp.arange(NA_ROWS)[:, None]).astype(F32)
    t = jnp.stack([
        sum(by_dr[:, j - cls + NA_ROWS - 1] * row_sel[j] for j in range(NA_ROWS))
        for cls in range(NA_ROWS)])
    t = jnp.where(col_ok[None, None], t, NEG_INF)
    return t.reshape(NA_ROWS, N_GROUPS, HEADS_PER_GROUP * GRID_W, SEG_KEYS)


LATENT_TILE = 1024
CTX_TILE = 512
COND_ROWS = 16


def kernel(x, c, ctx, c_ctx, w_ada, b_ada, norm_mix, norm_ffn, conv_w_in, conv_w,
           conv_w_out, attn_w_qkv, attn_q_norm, attn_k_norm, attn_rpb, attn_w_out,
           ffn_w_in, ffn_w_out):
    bsz, n, d = x.shape
    n_ctx = ctx.shape[1]
    assert d == D_MODEL and n % (ROWS_PER_STEP * GRID_W) == 0 and n % LATENT_TILE == 0
    assert n // GRID_W >= KEY_ROWS_PER_STEP and n_ctx % HALO == 0
    assert (bsz * n_ctx) % CTX_TILE == 0 and bsz < COND_ROWS
    flat = (1, bsz * n_ctx, d)
    per_batch = (bsz, n_ctx, d)
    ctx_row = bsz

    cond = jnp.concatenate(
        [c, c_ctx[None], jnp.zeros((COND_ROWS - bsz - 1, d), F32)], axis=0)
    mod = _ada(cond, w_ada, b_ada).reshape(DEPTH, COND_ROWS, 6, d)

    g_mix = norm_mix.reshape(DEPTH, 1, d)
    g_ffn = norm_ffn.reshape(DEPTH, 1, d)
    gq = jnp.tile(attn_q_norm, (1, HEADS_PER_GROUP))[:, None]
    gk = jnp.tile(attn_k_norm, (1, HEADS_PER_GROUP))[:, None]

    def matmul_params(i):
        mix_in, mix_out = (conv_w_in, conv_w_out) if i % 2 == 0 else (attn_w_qkv, attn_w_out)
        return [(mix_in, i // 2), (mix_out, i // 2), (ffn_w_in, i), (ffn_w_out, i)]

    m_in, m_out, f_in, f_out = _cast_params(matmul_params(0))

    for i in range(DEPTH):
        update_ctx = i < DEPTH - 1
        j = i // 2
        lat = dict(layer=i, cond_row=None)
        con = dict(layer=i, cond_row=ctx_row)
        cast = matmul_params(i + 1) if i + 1 < DEPTH else ()
        if i % 2 == 0:
            x = _conv_mixer(x, mod, g_mix, m_in, conv_w, m_out, tm=LATENT_TILE, mixer=j, **lat)
            x, nxt = _ffn(x, mod, g_ffn, f_in, f_out, tm=LATENT_TILE, cast=cast, **lat)
            if update_ctx:
                ctx = _conv_mixer(ctx, mod, g_mix, m_in, conv_w, m_out, tm=n_ctx, mixer=j, **con)
                ctx = _ffn(ctx.reshape(flat), mod, g_ffn, f_in, f_out, tm=CTX_TILE,
                           **con)[0].reshape(per_batch)
        else:
            q, k, v = _qkv(x, mod, g_mix, m_in, gq, gk, tm=LATENT_TILE, mixer=j,
                           keys_by_col_block=True, **lat)
            qc, kc, vc = (t.reshape(per_batch) for t in _qkv(
                ctx.reshape(flat), mod, g_mix, m_in, gq, gk, tm=CTX_TILE, mixer=j,
                keys_by_col_block=False, **con))
            o = _nbr_attn(q, k, v, kc, vc, _bias_table(attn_rpb[j]))
            x, nxt = _ffn(x, mod, g_ffn, f_in, f_out, tm=LATENT_TILE, attn_o=o, w_o=m_out,
                          cast=cast, **lat)
            if update_ctx:
                oc = _ctx_attn(qc, kc, vc)
                ctx = _ffn(ctx.reshape(flat), mod, g_ffn, f_in, f_out, tm=CTX_TILE,
                           attn_o=oc.reshape(flat), w_o=m_out, **con)[0].reshape(per_batch)
        if nxt:
            m_in, m_out, f_in, f_out = nxt
    return x
```

```python
import functools

import jax
import jax.numpy as jnp
from jax import lax
from jax.experimental import pallas as pl
from jax.experimental.pallas import tpu as pltpu

D_MODEL = 1024
DEPTH = 4
GRID_W = 64
N_HEADS = 16
HEAD_DIM = D_MODEL // N_HEADS
NA_ROWS = 8
NA_COLS = 16
FFN_HIDDEN = 2816
EPS = 1e-6
NEG_INF = -1e30

F32 = jnp.float32
BF16 = jnp.bfloat16

VMEM_LIMIT_BYTES = 56 * 1024 * 1024
HEADS_PER_GROUP = 4
GROUP_W = HEADS_PER_GROUP * HEAD_DIM
N_GROUPS = N_HEADS // HEADS_PER_GROUP
ROWS_PER_STEP = 8
KEY_ROWS_PER_STEP = ROWS_PER_STEP + NA_ROWS
COL_BLOCK = NA_COLS
N_COL_BLOCKS = GRID_W // COL_BLOCK
WIN_BLOCK = NA_ROWS * COL_BLOCK
QUERY_TILE = 8


def _query_segments():
    segs = []
    for q0 in range(0, GRID_W, QUERY_TILE):
        first = min(max(q0 - NA_COLS // 2, 0), GRID_W - NA_COLS)
        last = min(max(q0 + QUERY_TILE - 1 - NA_COLS // 2, 0), GRID_W - NA_COLS) + NA_COLS
        kb0 = min(first // COL_BLOCK, N_COL_BLOCKS - 2)
        assert last <= (kb0 + 2) * COL_BLOCK
        if segs and segs[-1][2] == kb0:
            segs[-1] = (segs[-1][0], q0 + QUERY_TILE, kb0)
        else:
            segs.append((q0, q0 + QUERY_TILE, kb0))
    return tuple(segs)


QUERY_SEGMENTS = _query_segments()
SEG_KEYS = 2 * WIN_BLOCK


def _params(*sem):
    return pltpu.CompilerParams(dimension_semantics=sem,
                                vmem_limit_bytes=VMEM_LIMIT_BYTES)


def _const_spec(shape):
    nd = len(shape)
    return pl.BlockSpec(shape, lambda *_: (0,) * nd, pipeline_mode=pl.Buffered(1))


def _layer_spec(shape, layer):
    nd = len(shape)
    return pl.BlockSpec((None,) + tuple(shape), lambda *_: (layer,) + (0,) * nd,
                        pipeline_mode=pl.Buffered(1))


def _mod_spec(layer, cond_row):
    if cond_row is None:
        return pl.BlockSpec((None, None, 6, D_MODEL), lambda b, i: (layer, b, 0, 0))
    return pl.BlockSpec((None, None, 6, D_MODEL), lambda b, i: (layer, cond_row, 0, 0))


def _silu(x):
    return x / (1.0 + jnp.exp(-x))


def _norm_mod(x, gain_scale, shift):
    ms = jnp.mean(x * x, axis=-1, keepdims=True)
    return x * lax.rsqrt(ms + EPS) * gain_scale + shift


def _dot(a, b):
    return jnp.dot(a, b, preferred_element_type=F32)


def _dot_nt(a, b):
    return lax.dot_general(a, b, (((1,), (1,)), ((), ())),
                           preferred_element_type=F32)


ADA_COLS = 1024


def _ada_kernel(c_ref, w_ref, b_ref, o_ref):
    a = _silu(c_ref[...]).astype(BF16)
    o_ref[0] = _dot(a, w_ref[0].astype(BF16)) + b_ref[0]


def _ada(cond, w_ada, b_ada):
    n_rows = cond.shape[0]
    n_cols = w_ada.shape[-1]
    return pl.pallas_call(
        _ada_kernel,
        grid=(DEPTH, n_cols // ADA_COLS),
        in_specs=[
            pl.BlockSpec((n_rows, D_MODEL), lambda l, j: (0, 0)),
            pl.BlockSpec((1, D_MODEL, ADA_COLS), lambda l, j: (l, 0, j)),
            pl.BlockSpec((1, 1, ADA_COLS), lambda l, j: (l, 0, j)),
        ],
        out_specs=pl.BlockSpec((1, n_rows, ADA_COLS), lambda l, j: (l, 0, j)),
        out_shape=jax.ShapeDtypeStruct((DEPTH, n_rows, n_cols), F32),
        compiler_params=_params("arbitrary", "arbitrary"),
        name="ada",
    )(cond, w_ada, b_ada.reshape(DEPTH, 1, n_cols))


CONV_CHUNK = 512
HALO = 8


def _conv_kernel(x_ref, xp_ref, xn_ref, mod_ref, gain_ref, win_ref, cw_ref,
                 wout_ref, o_ref, *, tm):
    i = pl.program_id(1)
    last = pl.num_programs(1) - 1
    m = mod_ref[...]
    shift = m[0:1]
    gain_scale = gain_ref[...] * (1.0 + m[1:2])
    gate = m[2:3]

    x = x_ref[0]
    x_ext = jnp.concatenate([x, xp_ref[0], xn_ref[0]], axis=0)
    h_ext = _norm_mod(x_ext, gain_scale, shift).astype(BF16)
    h = h_ext[:tm]
    has_prev = i > 0
    has_next = i < last

    cw = cw_ref[...]
    rows = lax.broadcasted_iota(jnp.int32, (tm, CONV_CHUNK), 0)
    gated = []
    for j in range(D_MODEL // CONV_CHUNK):
        lo, hi = j * CONV_CHUNK, (j + 1) * CONV_CHUNK
        b_gate = _dot(h, win_ref[:, lo:hi])
        c_gate = _dot(h_ext, win_ref[:, D_MODEL + lo:D_MODEL + hi])
        u = _dot(h_ext, win_ref[:, 2 * D_MODEL + lo:2 * D_MODEL + hi])
        z_ext = c_gate * u
        z = z_ext[:tm]
        z_prev = jnp.where(has_prev, z_ext[tm + HALO - 1:tm + HALO], 0.0)
        z_next = jnp.where(has_next, z_ext[tm + HALO:tm + HALO + 1], 0.0)
        z_up = jnp.where(rows == 0, z_prev, pltpu.roll(z, 1, 0))
        z_dn = jnp.where(rows == tm - 1, z_next, pltpu.roll(z, tm - 1, 0))
        zc = cw[0:1, lo:hi] * z_up + cw[1:2, lo:hi] * z + cw[2:3, lo:hi] * z_dn
        gated.append((b_gate * zc).astype(BF16))
    o_ref[0] = x + gate * _dot(jnp.concatenate(gated, axis=1), wout_ref[...])


def _conv_mixer(x, mod, gain, w_in, conv_w, w_out, *, tm, layer, mixer, cond_row):
    bsz, n, _ = x.shape
    nt = n // tm
    hb = tm // HALO
    n_hb = n // HALO
    return pl.pallas_call(
        functools.partial(_conv_kernel, tm=tm),
        grid=(bsz, nt),
        in_specs=[
            pl.BlockSpec((1, tm, D_MODEL), lambda b, i: (b, i, 0)),
            pl.BlockSpec((1, HALO, D_MODEL),
                         lambda b, i: (b, jnp.maximum(i * hb - 1, 0), 0)),
            pl.BlockSpec((1, HALO, D_MODEL),
                         lambda b, i: (b, jnp.minimum((i + 1) * hb, n_hb - 1), 0)),
            _mod_spec(layer, cond_row),
            _layer_spec((1, D_MODEL), layer),
            _layer_spec((D_MODEL, 3 * D_MODEL), 0),
            _layer_spec((3, D_MODEL), mixer),
            _layer_spec((D_MODEL, D_MODEL), 0),
        ],
        out_specs=pl.BlockSpec((1, tm, D_MODEL), lambda b, i: (b, i, 0)),
        out_shape=jax.ShapeDtypeStruct(x.shape, F32),
        compiler_params=_params("parallel", "arbitrary"),
        name="conv_mixer",
    )(x, x, x, mod, gain, w_in, conv_w, w_out)


FFN_CHUNK = 256
FFN_ROW_TILE = 256


def _ffn_kernel(*refs, with_attn_out, n_cast):
    n_in = len(refs) - 1 - 2 * n_cast
    cast_in = refs[n_in:n_in + n_cast]
    o_ref = refs[n_in + n_cast]
    cast_out = refs[n_in + n_cast + 1:]
    if with_attn_out:
        x_ref, ao_ref, wo_ref, mod_ref, gain_ref, win_ref, wout_ref = refs[:n_in]
    else:
        x_ref, mod_ref, gain_ref, win_ref, wout_ref = refs[:n_in]
    for src, dst in zip(cast_in, cast_out):
        dst[...] = src[...].astype(BF16)
    m = mod_ref[...]
    shift = m[3:4]
    gain_scale = gain_ref[...] * (1.0 + m[4:5])
    gate = m[5:6]
    x = x_ref[0]
    if with_attn_out:
        x = x + m[2:3] * _dot(ao_ref[0], wo_ref[...])
    h = _norm_mod(x, gain_scale, shift).astype(BF16)
    for r in range(0, x.shape[0], FFN_ROW_TILE):
        h_r = h[r:r + FFN_ROW_TILE]
        acts = []
        for lo in range(0, FFN_HIDDEN, FFN_CHUNK):
            hi = min(lo + FFN_CHUNK, FFN_HIDDEN)
            g = _dot(h_r, win_ref[:, lo:hi])
            u = _dot(h_r, win_ref[:, FFN_HIDDEN + lo:FFN_HIDDEN + hi])
            acts.append((_silu(g) * u).astype(BF16))
        y = _dot(jnp.concatenate(acts, axis=1), wout_ref[...])
        o_ref[0, r:r + FFN_ROW_TILE] = x[r:r + FFN_ROW_TILE] + gate * y


BF16_ROW_TILE = 16
LANE_TILE = 128


def _cast_blocking(r, c, steps):
    for cb in range(1, steps + 1):
        rb = steps // cb
        if (steps % cb == 0 and r % rb == 0 and c % cb == 0
                and (r // rb) % BF16_ROW_TILE == 0 and (c // cb) % LANE_TILE == 0):
            return rb, cb
    raise ValueError(f"no tile-aligned split of a ({r}, {c}) parameter over {steps} grid steps")


CAST_STEPS = 32


def _cast_kernel(*refs):
    n = len(refs) // 2
    for src, dst in zip(refs[:n], refs[n:]):
        dst[...] = src[...].astype(BF16)


def _cast_params(params):
    in_specs, out_specs, shapes = [], [], []
    for w, w_layer in params:
        _, r, c = w.shape
        rb, cb = _cast_blocking(r, c, CAST_STEPS)
        blk = (None, r // rb, c // cb)
        in_specs.append(pl.BlockSpec(
            blk, lambda s, w_layer=w_layer, cb=cb: (w_layer, s // cb, s % cb)))
        out_specs.append(pl.BlockSpec(blk, lambda s, cb=cb: (0, s // cb, s % cb)))
        shapes.append(jax.ShapeDtypeStruct((1, r, c), BF16))
    return pl.pallas_call(
        _cast_kernel,
        grid=(CAST_STEPS,),
        in_specs=in_specs,
        out_specs=out_specs,
        out_shape=shapes,
        compiler_params=_params("arbitrary"),
        name="cast_params",
    )(*[w for w, _ in params])


def _ffn(x, mod, gain, w_in, w_out, *, tm, layer, cond_row, attn_o=None, w_o=None, cast=()):
    bsz, n, _ = x.shape
    nt = n // tm
    steps = bsz * nt
    tok = pl.BlockSpec((1, tm, D_MODEL), lambda b, i: (b, i, 0))
    with_attn_out = attn_o is not None
    attn_args = (attn_o, w_o) if with_attn_out else ()
    attn_specs = [tok, _layer_spec((D_MODEL, D_MODEL), 0)] if with_attn_out else []
    cast_in_specs, cast_out_specs, cast_shapes = [], [], []
    for w, w_layer in cast:
        _, r, c = w.shape
        rb, cb = _cast_blocking(r, c, steps)
        cast_in_specs.append(pl.BlockSpec(
            (None, r // rb, c // cb),
            lambda b, i, w_layer=w_layer, cb=cb: (w_layer, (b * nt + i) // cb, (b * nt + i) % cb)))
        cast_out_specs.append(pl.BlockSpec(
            (None, r // rb, c // cb),
            lambda b, i, cb=cb: (0, (b * nt + i) // cb, (b * nt + i) % cb)))
        cast_shapes.append(jax.ShapeDtypeStruct((1, r, c), BF16))
    outs = pl.pallas_call(
        functools.partial(_ffn_kernel, with_attn_out=with_attn_out, n_cast=len(cast)),
        grid=(bsz, nt),
        in_specs=[tok] + attn_specs + [
            _mod_spec(layer, cond_row),
            _layer_spec((1, D_MODEL), layer),
            _layer_spec((D_MODEL, 2 * FFN_HIDDEN), 0),
            _layer_spec((FFN_HIDDEN, D_MODEL), 0),
        ] + cast_in_specs,
        out_specs=[tok] + cast_out_specs,
        out_shape=[jax.ShapeDtypeStruct(x.shape, F32)] + cast_shapes,
        compiler_params=_params("parallel", "arbitrary"),
        name="attn_out_ffn" if with_attn_out else "ffn",
    )(x, *attn_args, mod, gain, w_in, w_out, *[w for w, _ in cast])
    return outs[0], list(outs[1:])


QKV_CHUNK = 512


def _head_mean_matrix():
    r = lax.broadcasted_iota(jnp.int32, (GROUP_W, GROUP_W), 0) // HEAD_DIM
    c = lax.broadcasted_iota(jnp.int32, (GROUP_W, GROUP_W), 1) // HEAD_DIM
    return jnp.where(r == c, 1.0, 0.0).astype(BF16)


def _store_keys(ref, val, c0, by_col_block):
    tm, w = val.shape
    if not by_col_block:
        ref[0, :, c0:c0 + w] = val
        return
    for row in range(tm // GRID_W):
        for kb in range(N_COL_BLOCKS):
            src = row * GRID_W + kb * COL_BLOCK
            ref[0, kb, row * COL_BLOCK:(row + 1) * COL_BLOCK, c0:c0 + w] = val[src:src + COL_BLOCK]


def _qkv_kernel(x_ref, mod_ref, gain_ref, w_ref, gq_ref, gk_ref,
                q_ref, k_ref, v_ref, *, keys_by_col_block):
    m = mod_ref[...]
    shift = m[0:1]
    gain_scale = gain_ref[...] * (1.0 + m[1:2])
    h = _norm_mod(x_ref[0], gain_scale, shift).astype(BF16)
    ones_bd = _head_mean_matrix()
    gq = gq_ref[...] * (HEAD_DIM ** -0.5)
    gk = gk_ref[...]
    for j in range(D_MODEL // QKV_CHUNK):
        lo = j * QKV_CHUNK
        for base, g, out, permute in ((0, gq, q_ref, False),
                                      (D_MODEL, gk, k_ref, keys_by_col_block)):
            t2 = _dot(h, w_ref[:, base + lo:base + lo + QKV_CHUNK])
            for s in range(QKV_CHUNK // GROUP_W):
                t = t2[:, s * GROUP_W:(s + 1) * GROUP_W]
                ms = _dot((t * t).astype(BF16), ones_bd) * (1.0 / HEAD_DIM)
                normed = (t * lax.rsqrt(ms + EPS) * g).astype(BF16)
                _store_keys(out, normed, lo + s * GROUP_W, permute)
        v = _dot(h, w_ref[:, 2 * D_MODEL + lo:2 * D_MODEL + lo + QKV_CHUNK]).astype(BF16)
        _store_keys(v_ref, v, lo, keys_by_col_block)


def _qkv(x, mod, gain, w_qkv, gq, gk, *, tm, layer, mixer, cond_row, keys_by_col_block):
    bsz, n, _ = x.shape
    tok = pl.BlockSpec((1, tm, D_MODEL), lambda b, i: (b, i, 0))
    out = jax.ShapeDtypeStruct(x.shape, BF16)
    if keys_by_col_block:
        assert tm % GRID_W == 0
        kv_spec = pl.BlockSpec((1, N_COL_BLOCKS, tm // N_COL_BLOCKS, D_MODEL),
                               lambda b, i: (b, 0, i, 0))
        kv_out = jax.ShapeDtypeStruct((bsz, N_COL_BLOCKS, n // N_COL_BLOCKS, D_MODEL), BF16)
    else:
        kv_spec, kv_out = tok, out
    return pl.pallas_call(
        functools.partial(_qkv_kernel, keys_by_col_block=keys_by_col_block),
        grid=(bsz, n // tm),
        in_specs=[
            tok,
            _mod_spec(layer, cond_row),
            _layer_spec((1, D_MODEL), layer),
            _layer_spec((D_MODEL, 3 * D_MODEL), 0),
            _layer_spec((1, GROUP_W), mixer),
            _layer_spec((1, GROUP_W), mixer),
        ],
        out_specs=[tok, kv_spec, kv_spec],
        out_shape=[out, kv_out, kv_out],
        compiler_params=_params("parallel", "arbitrary"),
        name="qkv",
    )(x, mod, gain, w_qkv, gq, gk)


def _stack_heads(q):
    lane_head = lax.broadcasted_iota(jnp.int32, q.shape, 1) // HEAD_DIM
    zero = jnp.zeros_like(q)
    return jnp.concatenate(
        [jnp.where(lane_head == hh, q, zero) for hh in range(HEADS_PER_GROUP)], axis=0)


def _unstack_heads(o, n):
    lane_head = lax.broadcasted_iota(jnp.int32, (n, GROUP_W), 1) // HEAD_DIM
    out = o[0:n]
    for hh in range(1, HEADS_PER_GROUP):
        out = jnp.where(lane_head == hh, o[hh * n:(hh + 1) * n], out)
    return out


def _window_scores(s):
    pieces = []
    for hh in range(HEADS_PER_GROUP):
        for q0, q1, kb0 in QUERY_SEGMENTS:
            pieces.append(s[hh * GRID_W + q0:hh * GRID_W + q1,
                            kb0 * WIN_BLOCK:kb0 * WIN_BLOCK + SEG_KEYS])
    return jnp.concatenate(pieces, axis=0)


def _window_probs(p):
    pieces = []
    for hh in range(HEADS_PER_GROUP):
        for q0, q1, kb0 in QUERY_SEGMENTS:
            n = q1 - q0
            parts = [p[hh * GRID_W + q0:hh * GRID_W + q1]]
            if kb0 > 0:
                parts.insert(0, jnp.zeros((n, kb0 * WIN_BLOCK), p.dtype))
            n_after = N_COL_BLOCKS - 2 - kb0
            if n_after > 0:
                parts.append(jnp.zeros((n, n_after * WIN_BLOCK), p.dtype))
            pieces.append(jnp.concatenate(parts, axis=1))
    return jnp.concatenate(pieces, axis=0)


def _nbr_attn_kernel(q_ref, k_ref, v_ref, kc_ref, vc_ref, bias_ref, o_ref):
    rows = pl.num_programs(1) * ROWS_PER_STEP
    r0 = pl.program_id(1) * ROWS_PER_STEP
    win0 = jnp.clip(r0 - NA_ROWS // 2, 0, rows - KEY_ROWS_PER_STEP)
    n_win = N_COL_BLOCKS * WIN_BLOCK

    for i in range(ROWS_PER_STEP):
        r = r0 + i
        r_start = jnp.clip(r - NA_ROWS // 2, 0, rows - NA_ROWS)
        cls = r - r_start
        koff = pl.multiple_of((r_start - win0) * COL_BLOCK, COL_BLOCK)
        qoff = i * GRID_W
        for g in range(N_GROUPS):
            lo, hi = g * GROUP_W, (g + 1) * GROUP_W
            q = _stack_heads(q_ref[0, pl.ds(qoff, GRID_W), lo:hi])
            k_all = jnp.concatenate(
                [k_ref[0, kb, pl.ds(koff, WIN_BLOCK), lo:hi] for kb in range(N_COL_BLOCKS)]
                + [kc_ref[0, :, lo:hi]], axis=0)
            v_all = jnp.concatenate(
                [v_ref[0, kb, pl.ds(koff, WIN_BLOCK), lo:hi] for kb in range(N_COL_BLOCKS)]
                + [vc_ref[0, :, lo:hi]], axis=0)
            s_all = _dot_nt(q, k_all)
            s_win = _window_scores(s_all[:, :n_win]) + bias_ref[cls, g]
            s = jnp.concatenate([s_win, s_all[:, n_win:]], axis=1)
            p = jnp.exp(s - jnp.max(s, axis=-1, keepdims=True))
            denom = jnp.sum(p, axis=-1, keepdims=True)
            p_all = jnp.concatenate([_window_probs(p[:, :SEG_KEYS]), p[:, SEG_KEYS:]], axis=1)
            pv = _dot(p_all.astype(BF16), v_all)
            o = _unstack_heads(pv / denom, GRID_W)
            o_ref[0, pl.ds(qoff, GRID_W), lo:hi] = o.astype(BF16)


def _nbr_attn(q, k, v, kc, vc, bias):
    bsz, n, _ = q.shape
    rows = n // GRID_W
    tq = ROWS_PER_STEP * GRID_W
    n_ctx = kc.shape[1]
    assert k.shape == (bsz, N_COL_BLOCKS, rows * COL_BLOCK, D_MODEL)

    def key_map(b, i):
        start = jnp.clip(i * ROWS_PER_STEP - NA_ROWS // 2, 0, rows - KEY_ROWS_PER_STEP)
        return (b, 0, start * COL_BLOCK, 0)

    key_spec = pl.BlockSpec(
        (pl.Element(1), pl.Element(N_COL_BLOCKS), pl.Element(KEY_ROWS_PER_STEP * COL_BLOCK),
         pl.Element(D_MODEL)), key_map)
    ctx_spec = pl.BlockSpec((1, n_ctx, D_MODEL), lambda b, i: (b, 0, 0))
    return pl.pallas_call(
        _nbr_attn_kernel,
        grid=(bsz, rows // ROWS_PER_STEP),
        in_specs=[
            pl.BlockSpec((1, tq, D_MODEL), lambda b, i: (b, i, 0)),
            key_spec, key_spec, ctx_spec, ctx_spec,
            _const_spec(bias.shape),
        ],
        out_specs=pl.BlockSpec((1, tq, D_MODEL), lambda b, i: (b, i, 0)),
        out_shape=jax.ShapeDtypeStruct(q.shape, BF16),
        compiler_params=_params("parallel", "arbitrary"),
        name="nbr_attn",
    )(q, k, v, kc, vc, bias)


def _ctx_attn_kernel(q_ref, k_ref, v_ref, o_ref):
    n = q_ref.shape[1]
    q = _stack_heads(q_ref[0])
    s = _dot_nt(q, k_ref[0])
    p = jnp.exp(s - jnp.max(s, axis=-1, keepdims=True))
    denom = jnp.sum(p, axis=-1, keepdims=True)
    pv = _dot(p.astype(BF16), v_ref[0])
    o_ref[0] = _unstack_heads(pv / denom, n).astype(BF16)


def _ctx_attn(q, k, v):
    bsz, n, _ = q.shape
    spec = pl.BlockSpec((1, n, GROUP_W), lambda b, g: (b, 0, g))
    return pl.pallas_call(
        _ctx_attn_kernel,
        grid=(bsz, N_GROUPS),
        in_specs=[spec, spec, spec],
        out_specs=spec,
        out_shape=jax.ShapeDtypeStruct(q.shape, BF16),
        compiler_params=_params("parallel", "arbitrary"),
        name="ctx_attn",
    )(q, k, v)


def _bias_table(rpb):
    q = jnp.arange(GRID_W)[:, None]
    lane = jnp.arange(SEG_KEYS)[None, :]
    kb0 = jnp.concatenate([jnp.full((q1 - q0,), s_kb0) for q0, q1, s_kb0 in QUERY_SEGMENTS])
    k = (kb0[:, None] + lane // WIN_BLOCK) * COL_BLOCK + lane % COL_BLOCK
    j_of_lane = (lane[0] % WIN_BLOCK) // COL_BLOCK
    k_start = jnp.clip(q - NA_COLS // 2, 0, GRID_W - NA_COLS)
    col_ok = (k >= k_start) & (k < k_start + NA_COLS)
    dc = jnp.clip(k - q + NA_COLS - 1, 0, 2 * NA_COLS - 2)
    onehot = (dc[None] == jnp.arange(2 * NA_COLS - 1)[:, None, None]).astype(F32)
    by_dr = jnp.einsum("hrc,cql->hrql", rpb.astype(F32), onehot,
                       precision=lax.Precision.HIGHEST)
    row_sel = (j_of_lane[None] == jnp.arange(NA_ROWS)[:, None]).astype(F32)
    t = jnp.stack([
        sum(by_dr[:, j - cls + NA_ROWS - 1] * row_sel[j] for j in range(NA_ROWS))
        for cls in range(NA_ROWS)])
    t = jnp.where(col_ok[None, None], t, NEG_INF)
    return t.reshape(NA_ROWS, N_GROUPS, HEADS_PER_GROUP * GRID_W, SEG_KEYS)


LATENT_TILE = 1024
CTX_TILE = 512
COND_ROWS = 16


def kernel(x, c, ctx, c_ctx, w_ada, b_ada, norm_mix, norm_ffn, conv_w_in, conv_w,
           conv_w_out, attn_w_qkv, attn_q_norm, attn_k_norm, attn_rpb, attn_w_out,
           ffn_w_in, ffn_w_out):
    bsz, n, d = x.shape
    n_ctx = ctx.shape[1]
    assert d == D_MODEL and n % (ROWS_PER_STEP * GRID_W) == 0 and n % LATENT_TILE == 0
    assert n // GRID_W >= KEY_ROWS_PER_STEP and n_ctx % HALO == 0
    assert (bsz * n_ctx) % CTX_TILE == 0 and bsz < COND_ROWS
    flat = (1, bsz * n_ctx, d)
    per_batch = (bsz, n_ctx, d)
    ctx_row = bsz

    cond = jnp.concatenate(
        [c, c_ctx[None], jnp.zeros((COND_ROWS - bsz - 1, d), F32)], axis=0)
    mod = _ada(cond, w_ada, b_ada).reshape(DEPTH, COND_ROWS, 6, d)

    g_mix = norm_mix.reshape(DEPTH, 1, d)
    g_ffn = norm_ffn.reshape(DEPTH, 1, d)
    gq = jnp.tile(attn_q_norm, (1, HEADS_PER_GROUP))[:, None]
    gk = jnp.tile(attn_k_norm, (1, HEADS_PER_GROUP))[:, None]

    def matmul_params(i):
        mix_in, mix_out = (conv_w_in, conv_w_out) if i % 2 == 0 else (attn_w_qkv, attn_w_out)
        return [(mix_in, i // 2), (mix_out, i // 2), (ffn_w_in, i), (ffn_w_out, i)]

    m_in, m_out, f_in, f_out = _cast_params(matmul_params(0))

    for i in range(DEPTH):
        update_ctx = i < DEPTH - 1
        j = i // 2
        lat = dict(layer=i, cond_row=None)
        con = dict(layer=i, cond_row=ctx_row)
        cast = matmul_params(i + 1) if i + 1 < DEPTH else ()
        if i % 2 == 0:
            x = _conv_mixer(x, mod, g_mix, m_in, conv_w, m_out, tm=LATENT_TILE, mixer=j, **lat)
            x, nxt = _ffn(x, mod, g_ffn, f_in, f_out, tm=LATENT_TILE, cast=cast, **lat)
            if update_ctx:
                ctx = _conv_mixer(ctx, mod, g_mix, m_in, conv_w, m_out, tm=n_ctx, mixer=j, **con)
                ctx = _ffn(ctx.reshape(flat), mod, g_ffn, f_in, f_out, tm=CTX_TILE,
                           **con)[0].reshape(per_batch)
        else:
            q, k, v = _qkv(x, mod, g_mix, m_in, gq, gk, tm=LATENT_TILE, mixer=j,
                           keys_by_col_block=True, **lat)
            qc, kc, vc = (t.reshape(per_batch) for t in _qkv(
                ctx.reshape(flat), mod, g_mix, m_in, gq, gk, tm=CTX_TILE, mixer=j,
                keys_by_col_block=False, **con))
            o = _nbr_attn(q, k, v, kc, vc, _bias_table(attn_rpb[j]))
            x, nxt = _ffn(x, mod, g_ffn, f_in, f_out, tm=LATENT_TILE, attn_o=o, w_o=m_out,
                          cast=cast, **lat)
            if update_ctx:
                oc = _ctx_attn(qc, kc, vc)
                ctx = _ffn(ctx.reshape(flat), mod, g_ffn, f_in, f_out, tm=CTX_TILE,
                           attn_o=oc.reshape(flat), w_o=m_out, **con)[0].reshape(per_batch)
        if nxt:
            m_in, m_out, f_in, f_out = nxt
    return x
```

```python
import functools

import jax
import jax.numpy as jnp
from jax import lax
from jax.experimental import pallas as pl
from jax.experimental.pallas import tpu as pltpu

D_MODEL = 1024
DEPTH = 4
GRID_W = 64
N_HEADS = 16
HEAD_DIM = D_MODEL // N_HEADS
NA_ROWS = 8
NA_COLS = 16
FFN_HIDDEN = 2816
EPS = 1e-6
NEG_INF = -1e30

F32 = jnp.float32
BF16 = jnp.bfloat16

VMEM_LIMIT_BYTES = 56 * 1024 * 1024
HEADS_PER_GROUP = 4
GROUP_W = HEADS_PER_GROUP * HEAD_DIM
N_GROUPS = N_HEADS // HEADS_PER_GROUP
ROWS_PER_STEP = 8
KEY_ROWS_PER_STEP = ROWS_PER_STEP + NA_ROWS
COL_BLOCK = NA_COLS
N_COL_BLOCKS = GRID_W // COL_BLOCK
WIN_BLOCK = NA_ROWS * COL_BLOCK
QUERY_TILE = 8


def _query_segments():
    segs = []
    for q0 in range(0, GRID_W, QUERY_TILE):
        first = min(max(q0 - NA_COLS // 2, 0), GRID_W - NA_COLS)
        last = min(max(q0 + QUERY_TILE - 1 - NA_COLS // 2, 0), GRID_W - NA_COLS) + NA_COLS
        kb0 = min(first // COL_BLOCK, N_COL_BLOCKS - 2)
        assert last <= (kb0 + 2) * COL_BLOCK
        if segs and segs[-1][2] == kb0:
            segs[-1] = (segs[-1][0], q0 + QUERY_TILE, kb0)
        else:
            segs.append((q0, q0 + QUERY_TILE, kb0))
    return tuple(segs)


QUERY_SEGMENTS = _query_segments()
SEG_KEYS = 2 * WIN_BLOCK


def _params(*sem):
    return pltpu.CompilerParams(dimension_semantics=sem,
                                vmem_limit_bytes=VMEM_LIMIT_BYTES)


def _const_spec(shape):
    nd = len(shape)
    return pl.BlockSpec(shape, lambda *_: (0,) * nd, pipeline_mode=pl.Buffered(1))


def _layer_spec(shape, layer):
    nd = len(shape)
    return pl.BlockSpec((None,) + tuple(shape), lambda *_: (layer,) + (0,) * nd,
                        pipeline_mode=pl.Buffered(1))


def _mod_spec(layer, cond_row):
    if cond_row is None:
        return pl.BlockSpec((None, None, 6, D_MODEL), lambda b, i: (layer, b, 0, 0))
    return pl.BlockSpec((None, None, 6, D_MODEL), lambda b, i: (layer, cond_row, 0, 0))


def _silu(x):
    return x / (1.0 + jnp.exp(-x))


def _norm_mod(x, gain_scale, shift):
    ms = jnp.mean(x * x, axis=-1, keepdims=True)
    return x * lax.rsqrt(ms + EPS) * gain_scale + shift


def _dot(a, b):
    return jnp.dot(a, b, preferred_element_type=F32)


def _dot_nt(a, b):
    return lax.dot_general(a, b, (((1,), (1,)), ((), ())),
                           preferred_element_type=F32)


ADA_COLS = 2048


def _ada_kernel(c_ref, w_ref, b_ref, o_ref):
    a = _silu(c_ref[...]).astype(BF16)
    o_ref[0] = _dot(a, w_ref[0].astype(BF16)) + b_ref[0]


def _ada(cond, w_ada, b_ada):
    n_rows = cond.shape[0]
    n_cols = w_ada.shape[-1]
    return pl.pallas_call(
        _ada_kernel,
        grid=(DEPTH, n_cols // ADA_COLS),
        in_specs=[
            pl.BlockSpec((n_rows, D_MODEL), lambda l, j: (0, 0)),
            pl.BlockSpec((1, D_MODEL, ADA_COLS), lambda l, j: (l, 0, j)),
            pl.BlockSpec((1, 1, ADA_COLS), lambda l, j: (l, 0, j)),
        ],
        out_specs=pl.BlockSpec((1, n_rows, ADA_COLS), lambda l, j: (l, 0, j)),
        out_shape=jax.ShapeDtypeStruct((DEPTH, n_rows, n_cols), F32),
        compiler_params=_params("arbitrary", "arbitrary"),
        name="ada",
    )(cond, w_ada, b_ada.reshape(DEPTH, 1, n_cols))


CONV_CHUNK = 512
HALO = 8


def _conv_kernel(x_ref, xp_ref, xn_ref, mod_ref, gain_ref, win_ref, cw_ref,
                 wout_ref, o_ref, *, tm):
    i = pl.program_id(1)
    last = pl.num_programs(1) - 1
    m = mod_ref[...]
    shift = m[0:1]
    gain_scale = gain_ref[...] * (1.0 + m[1:2])
    gate = m[2:3]

    x = x_ref[0]
    x_ext = jnp.concatenate([x, xp_ref[0], xn_ref[0]], axis=0)
    h_ext = _norm_mod(x_ext, gain_scale, shift).astype(BF16)
    h = h_ext[:tm]
    has_prev = i > 0
    has_next = i < last

    cw = cw_ref[...]
    rows = lax.broadcasted_iota(jnp.int32, (tm, CONV_CHUNK), 0)
    gated = []
    for j in range(D_MODEL // CONV_CHUNK):
        lo, hi = j * CONV_CHUNK, (j + 1) * CONV_CHUNK
        b_gate = _dot(h, win_ref[:, lo:hi])
        c_gate = _dot(h_ext, win_ref[:, D_MODEL + lo:D_MODEL + hi])
        u = _dot(h_ext, win_ref[:, 2 * D_MODEL + lo:2 * D_MODEL + hi])
        z_ext = c_gate * u
        z = z_ext[:tm]
        z_prev = jnp.where(has_prev, z_ext[tm + HALO - 1:tm + HALO], 0.0)
        z_next = jnp.where(has_next, z_ext[tm + HALO:tm + HALO + 1], 0.0)
        z_up = jnp.where(rows == 0, z_prev, pltpu.roll(z, 1, 0))
        z_dn = jnp.where(rows == tm - 1, z_next, pltpu.roll(z, tm - 1, 0))
        zc = cw[0:1, lo:hi] * z_up + cw[1:2, lo:hi] * z + cw[2:3, lo:hi] * z_dn
        gated.append((b_gate * zc).astype(BF16))
    o_ref[0] = x + gate * _dot(jnp.concatenate(gated, axis=1), wout_ref[...])


def _conv_mixer(x, mod, gain, w_in, conv_w, w_out, *, tm, layer, mixer, cond_row):
    bsz, n, _ = x.shape
    nt = n // tm
    hb = tm // HALO
    n_hb = n // HALO
    return pl.pallas_call(
        functools.partial(_conv_kernel, tm=tm),
        grid=(bsz, nt),
        in_specs=[
            pl.BlockSpec((1, tm, D_MODEL), lambda b, i: (b, i, 0)),
            pl.BlockSpec((1, HALO, D_MODEL),
                         lambda b, i: (b, jnp.maximum(i * hb - 1, 0), 0)),
            pl.BlockSpec((1, HALO, D_MODEL),
                         lambda b, i: (b, jnp.minimum((i + 1) * hb, n_hb - 1), 0)),
            _mod_spec(layer, cond_row),
            _layer_spec((1, D_MODEL), layer),
            _layer_spec((D_MODEL, 3 * D_MODEL), 0),
            _layer_spec((3, D_MODEL), mixer),
            _layer_spec((D_MODEL, D_MODEL), 0),
        ],
        out_specs=pl.BlockSpec((1, tm, D_MODEL), lambda b, i: (b, i, 0)),
        out_shape=jax.ShapeDtypeStruct(x.shape, F32),
        compiler_params=_params("parallel", "arbitrary"),
        name="conv_mixer",
    )(x, x, x, mod, gain, w_in, conv_w, w_out)


FFN_CHUNK = 256
FFN_ROW_TILE = 256


def _ffn_kernel(*refs, with_attn_out, n_cast):
    n_in = len(refs) - 1 - 2 * n_cast
    cast_in = refs[n_in:n_in + n_cast]
    o_ref = refs[n_in + n_cast]
    cast_out = refs[n_in + n_cast + 1:]
    if with_attn_out:
        x_ref, ao_ref, wo_ref, mod_ref, gain_ref, win_ref, wout_ref = refs[:n_in]
    else:
        x_ref, mod_ref, gain_ref, win_ref, wout_ref = refs[:n_in]
    for src, dst in zip(cast_in, cast_out):
        dst[...] = src[...].astype(BF16)
    m = mod_ref[...]
    shift = m[3:4]
    gain_scale = gain_ref[...] * (1.0 + m[4:5])
    gate = m[5:6]
    x = x_ref[0]
    if with_attn_out:
        x = x + m[2:3] * _dot(ao_ref[0], wo_ref[...])
    h = _norm_mod(x, gain_scale, shift).astype(BF16)
    for r in range(0, x.shape[0], FFN_ROW_TILE):
        h_r = h[r:r + FFN_ROW_TILE]
        acts = []
        for lo in range(0, FFN_HIDDEN, FFN_CHUNK):
            hi = min(lo + FFN_CHUNK, FFN_HIDDEN)
            g = _dot(h_r, win_ref[:, lo:hi])
            u = _dot(h_r, win_ref[:, FFN_HIDDEN + lo:FFN_HIDDEN + hi])
            acts.append((_silu(g) * u).astype(BF16))
        y = _dot(jnp.concatenate(acts, axis=1), wout_ref[...])
        o_ref[0, r:r + FFN_ROW_TILE] = x[r:r + FFN_ROW_TILE] + gate * y


BF16_ROW_TILE = 16
LANE_TILE = 128


def _cast_blocking(r, c, steps):
    for cb in range(1, steps + 1):
        rb = steps // cb
        if (steps % cb == 0 and r % rb == 0 and c % cb == 0
                and (r // rb) % BF16_ROW_TILE == 0 and (c // cb) % LANE_TILE == 0):
            return rb, cb
    raise ValueError(f"no tile-aligned split of a ({r}, {c}) parameter over {steps} grid steps")


CAST_STEPS = 16


def _cast_kernel(*refs):
    n = len(refs) // 2
    for src, dst in zip(refs[:n], refs[n:]):
        dst[...] = src[...].astype(BF16)


def _cast_params(params):
    in_specs, out_specs, shapes = [], [], []
    for w, w_layer in params:
        _, r, c = w.shape
        rb, cb = _cast_blocking(r, c, CAST_STEPS)
        blk = (None, r // rb, c // cb)
        in_specs.append(pl.BlockSpec(
            blk, lambda s, w_layer=w_layer, cb=cb: (w_layer, s // cb, s % cb)))
        out_specs.append(pl.BlockSpec(blk, lambda s, cb=cb: (0, s // cb, s % cb)))
        shapes.append(jax.ShapeDtypeStruct((1, r, c), BF16))
    return pl.pallas_call(
        _cast_kernel,
        grid=(CAST_STEPS,),
        in_specs=in_specs,
        out_specs=out_specs,
        out_shape=shapes,
        compiler_params=_params("arbitrary"),
        name="cast_params",
    )(*[w for w, _ in params])


def _ffn(x, mod, gain, w_in, w_out, *, tm, layer, cond_row, attn_o=None, w_o=None, cast=()):
    bsz, n, _ = x.shape
    nt = n // tm
    steps = bsz * nt
    tok = pl.BlockSpec((1, tm, D_MODEL), lambda b, i: (b, i, 0))
    with_attn_out = attn_o is not None
    attn_args = (attn_o, w_o) if with_attn_out else ()
    attn_specs = [tok, _layer_spec((D_MODEL, D_MODEL), 0)] if with_attn_out else []
    cast_in_specs, cast_out_specs, cast_shapes = [], [], []
    for w, w_layer in cast:
        _, r, c = w.shape
        rb, cb = _cast_blocking(r, c, steps)
        cast_in_specs.append(pl.BlockSpec(
            (None, r // rb, c // cb),
            lambda b, i, w_layer=w_layer, cb=cb: (w_layer, (b * nt + i) // cb, (b * nt + i) % cb)))
        cast_out_specs.append(pl.BlockSpec(
            (None, r // rb, c // cb),
            lambda b, i, cb=cb: (0, (b * nt + i) // cb, (b * nt + i) % cb)))
        cast_shapes.append(jax.ShapeDtypeStruct((1, r, c), BF16))
    outs = pl.pallas_call(
        functools.partial(_ffn_kernel, with_attn_out=with_attn_out, n_cast=len(cast)),
        grid=(bsz, nt),
        in_specs=[tok] + attn_specs + [
            _mod_spec(layer, cond_row),
            _layer_spec((1, D_MODEL), layer),
            _layer_spec((D_MODEL, 2 * FFN_HIDDEN), 0),
            _layer_spec((FFN_HIDDEN, D_MODEL), 0),
        ] + cast_in_specs,
        out_specs=[tok] + cast_out_specs,
        out_shape=[jax.ShapeDtypeStruct(x.shape, F32)] + cast_shapes,
        compiler_params=_params("parallel", "arbitrary"),
        name="attn_out_ffn" if with_attn_out else "ffn",
    )(x, *attn_args, mod, gain, w_in, w_out, *[w for w, _ in cast])
    return outs[0], list(outs[1:])


QKV_CHUNK = 512


def _head_mean_matrix():
    r = lax.broadcasted_iota(jnp.int32, (GROUP_W, GROUP_W), 0) // HEAD_DIM
    c = lax.broadcasted_iota(jnp.int32, (GROUP_W, GROUP_W), 1) // HEAD_DIM
    return jnp.where(r == c, 1.0, 0.0).astype(BF16)


def _store_keys(ref, val, c0, by_col_block):
    tm, w = val.shape
    if not by_col_block:
        ref[0, :, c0:c0 + w] = val
        return
    for row in range(tm // GRID_W):
        for kb in range(N_COL_BLOCKS):
            src = row * GRID_W + kb * COL_BLOCK
            ref[0, kb, row * COL_BLOCK:(row + 1) * COL_BLOCK, c0:c0 + w] = val[src:src + COL_BLOCK]


def _qkv_kernel(x_ref, mod_ref, gain_ref, w_ref, gq_ref, gk_ref,
                q_ref, k_ref, v_ref, *, keys_by_col_block):
    m = mod_ref[...]
    shift = m[0:1]
    gain_scale = gain_ref[...] * (1.0 + m[1:2])
    h = _norm_mod(x_ref[0], gain_scale, shift).astype(BF16)
    ones_bd = _head_mean_matrix()
    gq = gq_ref[...] * (HEAD_DIM ** -0.5)
    gk = gk_ref[...]
    for j in range(D_MODEL // QKV_CHUNK):
        lo = j * QKV_CHUNK
        for base, g, out, permute in ((0, gq, q_ref, False),
                                      (D_MODEL, gk, k_ref, keys_by_col_block)):
            t2 = _dot(h, w_ref[:, base + lo:base + lo + QKV_CHUNK])
            for s in range(QKV_CHUNK // GROUP_W):
                t = t2[:, s * GROUP_W:(s + 1) * GROUP_W]
                ms = _dot((t * t).astype(BF16), ones_bd) * (1.0 / HEAD_DIM)
                normed = (t * lax.rsqrt(ms + EPS) * g).astype(BF16)
                _store_keys(out, normed, lo + s * GROUP_W, permute)
        v = _dot(h, w_ref[:, 2 * D_MODEL + lo:2 * D_MODEL + lo + QKV_CHUNK]).astype(BF16)
        _store_keys(v_ref, v, lo, keys_by_col_block)


def _qkv(x, mod, gain, w_qkv, gq, gk, *, tm, layer, mixer, cond_row, keys_by_col_block):
    bsz, n, _ = x.shape
    tok = pl.BlockSpec((1, tm, D_MODEL), lambda b, i: (b, i, 0))
    out = jax.ShapeDtypeStruct(x.shape, BF16)
    if keys_by_col_block:
        assert tm % GRID_W == 0
        kv_spec = pl.BlockSpec((1, N_COL_BLOCKS, tm // N_COL_BLOCKS, D_MODEL),
                               lambda b, i: (b, 0, i, 0))
        kv_out = jax.ShapeDtypeStruct((bsz, N_COL_BLOCKS, n // N_COL_BLOCKS, D_MODEL), BF16)
    else:
        kv_spec, kv_out = tok, out
    return pl.pallas_call(
        functools.partial(_qkv_kernel, keys_by_col_block=keys_by_col_block),
        grid=(bsz, n // tm),
        in_specs=[
            tok,
            _mod_spec(layer, cond_row),
            _layer_spec((1, D_MODEL), layer),
            _layer_spec((D_MODEL, 3 * D_MODEL), 0),
            _layer_spec((1, GROUP_W), mixer),
            _layer_spec((1, GROUP_W), mixer),
        ],
        out_specs=[tok, kv_spec, kv_spec],
        out_shape=[out, kv_out, kv_out],
        compiler_params=_params("parallel", "arbitrary"),
        name="qkv",
    )(x, mod, gain, w_qkv, gq, gk)


def _stack_heads(q):
    lane_head = lax.broadcasted_iota(jnp.int32, q.shape, 1) // HEAD_DIM
    zero = jnp.zeros_like(q)
    return jnp.concatenate(
        [jnp.where(lane_head == hh, q, zero) for hh in range(HEADS_PER_GROUP)], axis=0)


def _unstack_heads(o, n):
    lane_head = lax.broadcasted_iota(jnp.int32, (n, GROUP_W), 1) // HEAD_DIM
    out = o[0:n]
    for hh in range(1, HEADS_PER_GROUP):
        out = jnp.where(lane_head == hh, o[hh * n:(hh + 1) * n], out)
    return out


def _window_scores(s):
    pieces = []
    for hh in range(HEADS_PER_GROUP):
        for q0, q1, kb0 in QUERY_SEGMENTS:
            pieces.append(s[hh * GRID_W + q0:hh * GRID_W + q1,
                            kb0 * WIN_BLOCK:kb0 * WIN_BLOCK + SEG_KEYS])
    return jnp.concatenate(pieces, axis=0)


def _window_probs(p):
    pieces = []
    for hh in range(HEADS_PER_GROUP):
        for q0, q1, kb0 in QUERY_SEGMENTS:
            n = q1 - q0
            parts = [p[hh * GRID_W + q0:hh * GRID_W + q1]]
            if kb0 > 0:
                parts.insert(0, jnp.zeros((n, kb0 * WIN_BLOCK), p.dtype))
            n_after = N_COL_BLOCKS - 2 - kb0
            if n_after > 0:
                parts.append(jnp.zeros((n, n_after * WIN_BLOCK), p.dtype))
            pieces.append(jnp.concatenate(parts, axis=1))
    return jnp.concatenate(pieces, axis=0)


def _nbr_attn_kernel(q_ref, k_ref, v_ref, kc_ref, vc_ref, bias_ref, o_ref):
    rows = pl.num_programs(1) * ROWS_PER_STEP
    r0 = pl.program_id(1) * ROWS_PER_STEP
    win0 = jnp.clip(r0 - NA_ROWS // 2, 0, rows - KEY_ROWS_PER_STEP)
    n_win = N_COL_BLOCKS * WIN_BLOCK

    for i in range(ROWS_PER_STEP):
        r = r0 + i
        r_start = jnp.clip(r - NA_ROWS // 2, 0, rows - NA_ROWS)
        cls = r - r_start
        koff = pl.multiple_of((r_start - win0) * COL_BLOCK, COL_BLOCK)
        qoff = i * GRID_W
        for g in range(N_GROUPS):
            lo, hi = g * GROUP_W, (g + 1) * GROUP_W
            q = _stack_heads(q_ref[0, pl.ds(qoff, GRID_W), lo:hi])
            k_all = jnp.concatenate(
                [k_ref[0, kb, pl.ds(koff, WIN_BLOCK), lo:hi] for kb in range(N_COL_BLOCKS)]
                + [kc_ref[0, :, lo:hi]], axis=0)
            v_all = jnp.concatenate(
                [v_ref[0, kb, pl.ds(koff, WIN_BLOCK), lo:hi] for kb in range(N_COL_BLOCKS)]
                + [vc_ref[0, :, lo:hi]], axis=0)
            s_all = _dot_nt(q, k_all)
            s_win = _window_scores(s_all[:, :n_win]) + bias_ref[cls, g]
            s = jnp.concatenate([s_win, s_all[:, n_win:]], axis=1)
            p = jnp.exp(s - jnp.max(s, axis=-1, keepdims=True))
            denom = jnp.sum(p, axis=-1, keepdims=True)
            p_all = jnp.concatenate([_window_probs(p[:, :SEG_KEYS]), p[:, SEG_KEYS:]], axis=1)
            pv = _dot(p_all.astype(BF16), v_all)
            o = _unstack_heads(pv / denom, GRID_W)
            o_ref[0, pl.ds(qoff, GRID_W), lo:hi] = o.astype(BF16)


def _nbr_attn(q, k, v, kc, vc, bias):
    bsz, n, _ = q.shape
    rows = n // GRID_W
    tq = ROWS_PER_STEP * GRID_W
    n_ctx = kc.shape[1]
    assert k.shape == (bsz, N_COL_BLOCKS, rows * COL_BLOCK, D_MODEL)

    def key_map(b, i):
        start = jnp.clip(i * ROWS_PER_STEP - NA_ROWS // 2, 0, rows - KEY_ROWS_PER_STEP)
        return (b, 0, start * COL_BLOCK, 0)

    key_spec = pl.BlockSpec(
        (pl.Element(1), pl.Element(N_COL_BLOCKS), pl.Element(KEY_ROWS_PER_STEP * COL_BLOCK),
         pl.Element(D_MODEL)), key_map)
    ctx_spec = pl.BlockSpec((1, n_ctx, D_MODEL), lambda b, i: (b, 0, 0))
    return pl.pallas_call(
        _nbr_attn_kernel,
        grid=(bsz, rows // ROWS_PER_STEP),
        in_specs=[
            pl.BlockSpec((1, tq, D_MODEL), lambda b, i: (b, i, 0)),
            key_spec, key_spec, ctx_spec, ctx_spec,
            _const_spec(bias.shape),
        ],
        out_specs=pl.BlockSpec((1, tq, D_MODEL), lambda b, i: (b, i, 0)),
        out_shape=jax.ShapeDtypeStruct(q.shape, BF16),
        compiler_params=_params("parallel", "arbitrary"),
        name="nbr_attn",
    )(q, k, v, kc, vc, bias)


def _ctx_attn_kernel(q_ref, k_ref, v_ref, o_ref):
    n = q_ref.shape[1]
    for g in range(N_GROUPS):
        lo, hi = g * GROUP_W, (g + 1) * GROUP_W
        q = _stack_heads(q_ref[0, :, lo:hi])
        s = _dot_nt(q, k_ref[0, :, lo:hi])
        p = jnp.exp(s - jnp.max(s, axis=-1, keepdims=True))
        denom = jnp.sum(p, axis=-1, keepdims=True)
        pv = _dot(p.astype(BF16), v_ref[0, :, lo:hi])
        o_ref[0, :, lo:hi] = _unstack_heads(pv / denom, n).astype(BF16)


def _ctx_attn(q, k, v):
    bsz, n, _ = q.shape
    spec = pl.BlockSpec((1, n, D_MODEL), lambda b: (b, 0, 0))
    return pl.pallas_call(
        _ctx_attn_kernel,
        grid=(bsz,),
        in_specs=[spec, spec, spec],
        out_specs=spec,
        out_shape=jax.ShapeDtypeStruct(q.shape, BF16),
        compiler_params=_params("parallel"),
        name="ctx_attn",
    )(q, k, v)


def _bias_table(rpb):
    q = jnp.arange(GRID_W)[:, None]
    lane = jnp.arange(SEG_KEYS)[None, :]
    kb0 = jnp.concatenate([jnp.full((q1 - q0,), s_kb0) for q0, q1, s_kb0 in QUERY_SEGMENTS])
    k = (kb0[:, None] + lane // WIN_BLOCK) * COL_BLOCK + lane % COL_BLOCK
    j_of_lane = (lane[0] % WIN_BLOCK) // COL_BLOCK
    k_start = jnp.clip(q - NA_COLS // 2, 0, GRID_W - NA_COLS)
    col_ok = (k >= k_start) & (k < k_start + NA_COLS)
    dc = jnp.clip(k - q + NA_COLS - 1, 0, 2 * NA_COLS - 2)
    onehot = (dc[None] == jnp.arange(2 * NA_COLS - 1)[:, None, None]).astype(F32)
    by_dr = jnp.einsum("hrc,cql->hrql", rpb.astype(F32), onehot,
                       precision=lax.Precision.HIGHEST)
    row_sel = (j_of_lane[None] == jnp.arange(NA_ROWS)[:, None]).astype(F32)
    t = jnp.stack([
        sum(by_dr[:, j - cls + NA_ROWS - 1] * row_sel[j] for j in range(NA_ROWS))
        for cls in range(NA_ROWS)])
    t = jnp.where(col_ok[None, None], t, NEG_INF)
    return t.reshape(NA_ROWS, N_GROUPS, HEADS_PER_GROUP * GRID_W, SEG_KEYS)


LATENT_TILE = 1024
CTX_TILE = 512
COND_ROWS = 16


def kernel(x, c, ctx, c_ctx, w_ada, b_ada, norm_mix, norm_ffn, conv_w_in, conv_w,
           conv_w_out, attn_w_qkv, attn_q_norm, attn_k_norm, attn_rpb, attn_w_out,
           ffn_w_in, ffn_w_out):
    bsz, n, d = x.shape
    n_ctx = ctx.shape[1]
    assert d == D_MODEL and n % (ROWS_PER_STEP * GRID_W) == 0 and n % LATENT_TILE == 0
    assert n // GRID_W >= KEY_ROWS_PER_STEP and n_ctx % HALO == 0
    assert (bsz * n_ctx) % CTX_TILE == 0 and bsz < COND_ROWS
    flat = (1, bsz * n_ctx, d)
    per_batch = (bsz, n_ctx, d)
    ctx_row = bsz

    cond = jnp.concatenate(
        [c, c_ctx[None], jnp.zeros((COND_ROWS - bsz - 1, d), F32)], axis=0)
    mod = _ada(cond, w_ada, b_ada).reshape(DEPTH, COND_ROWS, 6, d)

    g_mix = norm_mix.reshape(DEPTH, 1, d)
    g_ffn = norm_ffn.reshape(DEPTH, 1, d)
    gq = jnp.tile(attn_q_norm, (1, HEADS_PER_GROUP))[:, None]
    gk = jnp.tile(attn_k_norm, (1, HEADS_PER_GROUP))[:, None]

    def matmul_params(i):
        mix_in, mix_out = (conv_w_in, conv_w_out) if i % 2 == 0 else (attn_w_qkv, attn_w_out)
        return [(mix_in, i // 2), (mix_out, i // 2), (ffn_w_in, i), (ffn_w_out, i)]

    m_in, m_out, f_in, f_out = _cast_params(matmul_params(0))

    for i in range(DEPTH):
        update_ctx = i < DEPTH - 1
        j = i // 2
        lat = dict(layer=i, cond_row=None)
        con = dict(layer=i, cond_row=ctx_row)
        cast = matmul_params(i + 1) if i + 1 < DEPTH else ()
        if i % 2 == 0:
            x = _conv_mixer(x, mod, g_mix, m_in, conv_w, m_out, tm=LATENT_TILE, mixer=j, **lat)
            x, nxt = _ffn(x, mod, g_ffn, f_in, f_out, tm=LATENT_TILE, cast=cast, **lat)
            if update_ctx:
                ctx = _conv_mixer(ctx, mod, g_mix, m_in, conv_w, m_out, tm=n_ctx, mixer=j, **con)
                ctx = _ffn(ctx.reshape(flat), mod, g_ffn, f_in, f_out, tm=CTX_TILE,
                           **con)[0].reshape(per_batch)
        else:
            q, k, v = _qkv(x, mod, g_mix, m_in, gq, gk, tm=LATENT_TILE, mixer=j,
                           keys_by_col_block=True, **lat)
            qc, kc, vc = (t.reshape(per_batch) for t in _qkv(
                ctx.reshape(flat), mod, g_mix, m_in, gq, gk, tm=CTX_TILE, mixer=j,
                keys_by_col_block=False, **con))
            o = _nbr_attn(q, k, v, kc, vc, _bias_table(attn_rpb[j]))
            x, nxt = _ffn(x, mod, g_ffn, f_in, f_out, tm=LATENT_TILE, attn_o=o, w_o=m_out,
                          cast=cast, **lat)
            if update_ctx:
                oc = _ctx_attn(qc, kc, vc)
                ctx = _ffn(ctx.reshape(flat), mod, g_ffn, f_in, f_out, tm=CTX_TILE,
                           attn_o=oc.reshape(flat), w_o=m_out, **con)[0].reshape(per_batch)
        if nxt:
            m_in, m_out, f_in, f_out = nxt
    return x
```

```python
import functools

import jax
import jax.numpy as jnp
from jax import lax
from jax.experimental import pallas as pl
from jax.experimental.pallas import tpu as pltpu

D_MODEL = 1024
DEPTH = 4
GRID_W = 64
N_HEADS = 16
HEAD_DIM = D_MODEL // N_HEADS
NA_ROWS = 8
NA_COLS = 16
FFN_HIDDEN = 2816
EPS = 1e-6
NEG_INF = -1e30

F32 = jnp.float32
BF16 = jnp.bfloat16

VMEM_LIMIT_BYTES = 56 * 1024 * 1024
HEADS_PER_GROUP = 4
GROUP_W = HEADS_PER_GROUP * HEAD_DIM
N_GROUPS = N_HEADS // HEADS_PER_GROUP
ROWS_PER_STEP = 8
KEY_ROWS_PER_STEP = ROWS_PER_STEP + NA_ROWS
COL_BLOCK = NA_COLS
N_COL_BLOCKS = GRID_W // COL_BLOCK
WIN_BLOCK = NA_ROWS * COL_BLOCK
QUERY_TILE = 8


def _query_segments():
    segs = []
    for q0 in range(0, GRID_W, QUERY_TILE):
        first = min(max(q0 - NA_COLS // 2, 0), GRID_W - NA_COLS)
        last = min(max(q0 + QUERY_TILE - 1 - NA_COLS // 2, 0), GRID_W - NA_COLS) + NA_COLS
        kb0 = min(first // COL_BLOCK, N_COL_BLOCKS - 2)
        assert last <= (kb0 + 2) * COL_BLOCK
        if segs and segs[-1][2] == kb0:
            segs[-1] = (segs[-1][0], q0 + QUERY_TILE, kb0)
        else:
            segs.append((q0, q0 + QUERY_TILE, kb0))
    return tuple(segs)


QUERY_SEGMENTS = _query_segments()
SEG_KEYS = 2 * WIN_BLOCK


def _params(*sem):
    return pltpu.CompilerParams(dimension_semantics=sem,
                                vmem_limit_bytes=VMEM_LIMIT_BYTES)


def _const_spec(shape):
    nd = len(shape)
    return pl.BlockSpec(shape, lambda *_: (0,) * nd, pipeline_mode=pl.Buffered(1))


def _layer_spec(shape, layer):
    nd = len(shape)
    return pl.BlockSpec((None,) + tuple(shape), lambda *_: (layer,) + (0,) * nd,
                        pipeline_mode=pl.Buffered(1))


def _mod_spec(layer, cond_row):
    if cond_row is None:
        return pl.BlockSpec((None, None, 6, D_MODEL), lambda b, i: (layer, b, 0, 0))
    return pl.BlockSpec((None, None, 6, D_MODEL), lambda b, i: (layer, cond_row, 0, 0))


def _silu(x):
    return x / (1.0 + jnp.exp(-x))


def _norm_mod(x, gain_scale, shift):
    ms = jnp.mean(x * x, axis=-1, keepdims=True)
    return x * lax.rsqrt(ms + EPS) * gain_scale + shift


def _dot(a, b):
    return jnp.dot(a, b, preferred_element_type=F32)


def _dot_nt(a, b):
    return lax.dot_general(a, b, (((1,), (1,)), ((), ())),
                           preferred_element_type=F32)


ADA_COLS = 2048


def _ada_kernel(c_ref, w_ref, b_ref, o_ref):
    a = _silu(c_ref[...]).astype(BF16)
    o_ref[0] = _dot(a, w_ref[0].astype(BF16)) + b_ref[0]


def _ada(cond, w_ada, b_ada):
    n_rows = cond.shape[0]
    n_cols = w_ada.shape[-1]
    return pl.pallas_call(
        _ada_kernel,
        grid=(DEPTH, n_cols // ADA_COLS),
        in_specs=[
            pl.BlockSpec((n_rows, D_MODEL), lambda l, j: (0, 0)),
            pl.BlockSpec((1, D_MODEL, ADA_COLS), lambda l, j: (l, 0, j)),
            pl.BlockSpec((1, 1, ADA_COLS), lambda l, j: (l, 0, j)),
        ],
        out_specs=pl.BlockSpec((1, n_rows, ADA_COLS), lambda l, j: (l, 0, j)),
        out_shape=jax.ShapeDtypeStruct((DEPTH, n_rows, n_cols), F32),
        compiler_params=_params("arbitrary", "arbitrary"),
        name="ada",
    )(cond, w_ada, b_ada.reshape(DEPTH, 1, n_cols))


CONV_CHUNK = 256
HALO = 8


def _conv_kernel(x_ref, xp_ref, xn_ref, mod_ref, gain_ref, win_ref, cw_ref,
                 wout_ref, o_ref, *, tm):
    i = pl.program_id(1)
    last = pl.num_programs(1) - 1
    m = mod_ref[...]
    shift = m[0:1]
    gain_scale = gain_ref[...] * (1.0 + m[1:2])
    gate = m[2:3]

    x = x_ref[0]
    x_ext = jnp.concatenate([x, xp_ref[0], xn_ref[0]], axis=0)
    h_ext = _norm_mod(x_ext, gain_scale, shift).astype(BF16)
    h = h_ext[:tm]
    has_prev = i > 0
    has_next = i < last

    cw = cw_ref[...]
    rows = lax.broadcasted_iota(jnp.int32, (tm, CONV_CHUNK), 0)
    gated = []
    for j in range(D_MODEL // CONV_CHUNK):
        lo, hi = j * CONV_CHUNK, (j + 1) * CONV_CHUNK
        b_gate = _dot(h, win_ref[:, lo:hi])
        c_gate = _dot(h_ext, win_ref[:, D_MODEL + lo:D_MODEL + hi])
        u = _dot(h_ext, win_ref[:, 2 * D_MODEL + lo:2 * D_MODEL + hi])
        z_ext = c_gate * u
        z = z_ext[:tm]
        z_prev = jnp.where(has_prev, z_ext[tm + HALO - 1:tm + HALO], 0.0)
        z_next = jnp.where(has_next, z_ext[tm + HALO:tm + HALO + 1], 0.0)
        z_up = jnp.where(rows == 0, z_prev, pltpu.roll(z, 1, 0))
        z_dn = jnp.where(rows == tm - 1, z_next, pltpu.roll(z, tm - 1, 0))
        zc = cw[0:1, lo:hi] * z_up + cw[1:2, lo:hi] * z + cw[2:3, lo:hi] * z_dn
        gated.append((b_gate * zc).astype(BF16))
    o_ref[0] = x + gate * _dot(jnp.concatenate(gated, axis=1), wout_ref[...])


def _conv_mixer(x, mod, gain, w_in, conv_w, w_out, *, tm, layer, mixer, cond_row):
    bsz, n, _ = x.shape
    nt = n // tm
    hb = tm // HALO
    n_hb = n // HALO
    return pl.pallas_call(
        functools.partial(_conv_kernel, tm=tm),
        grid=(bsz, nt),
        in_specs=[
            pl.BlockSpec((1, tm, D_MODEL), lambda b, i: (b, i, 0)),
            pl.BlockSpec((1, HALO, D_MODEL),
                         lambda b, i: (b, jnp.maximum(i * hb - 1, 0), 0)),
            pl.BlockSpec((1, HALO, D_MODEL),
                         lambda b, i: (b, jnp.minimum((i + 1) * hb, n_hb - 1), 0)),
            _mod_spec(layer, cond_row),
            _layer_spec((1, D_MODEL), layer),
            _layer_spec((D_MODEL, 3 * D_MODEL), 0),
            _layer_spec((3, D_MODEL), mixer),
            _layer_spec((D_MODEL, D_MODEL), 0),
        ],
        out_specs=pl.BlockSpec((1, tm, D_MODEL), lambda b, i: (b, i, 0)),
        out_shape=jax.ShapeDtypeStruct(x.shape, F32),
        compiler_params=_params("parallel", "arbitrary"),
        name="conv_mixer",
    )(x, x, x, mod, gain, w_in, conv_w, w_out)


FFN_CHUNK = 256
FFN_ROW_TILE = 256


def _ffn_kernel(*refs, with_attn_out, n_cast):
    n_in = len(refs) - 1 - 2 * n_cast
    cast_in = refs[n_in:n_in + n_cast]
    o_ref = refs[n_in + n_cast]
    cast_out = refs[n_in + n_cast + 1:]
    if with_attn_out:
        x_ref, ao_ref, wo_ref, mod_ref, gain_ref, win_ref, wout_ref = refs[:n_in]
    else:
        x_ref, mod_ref, gain_ref, win_ref, wout_ref = refs[:n_in]
    for src, dst in zip(cast_in, cast_out):
        dst[...] = src[...].astype(BF16)
    m = mod_ref[...]
    shift = m[3:4]
    gain_scale = gain_ref[...] * (1.0 + m[4:5])
    gate = m[5:6]
    x = x_ref[0]
    if with_attn_out:
        x = x + m[2:3] * _dot(ao_ref[0], wo_ref[...])
    h = _norm_mod(x, gain_scale, shift).astype(BF16)
    for r in range(0, x.shape[0], FFN_ROW_TILE):
        h_r = h[r:r + FFN_ROW_TILE]
        acts = []
        for lo in range(0, FFN_HIDDEN, FFN_CHUNK):
            hi = min(lo + FFN_CHUNK, FFN_HIDDEN)
            g = _dot(h_r, win_ref[:, lo:hi])
            u = _dot(h_r, win_ref[:, FFN_HIDDEN + lo:FFN_HIDDEN + hi])
            acts.append((_silu(g) * u).astype(BF16))
        y = _dot(jnp.concatenate(acts, axis=1), wout_ref[...])
        o_ref[0, r:r + FFN_ROW_TILE] = x[r:r + FFN_ROW_TILE] + gate * y


BF16_ROW_TILE = 16
LANE_TILE = 128


def _cast_blocking(r, c, steps):
    for cb in range(1, steps + 1):
        rb = steps // cb
        if (steps % cb == 0 and r % rb == 0 and c % cb == 0
                and (r // rb) % BF16_ROW_TILE == 0 and (c // cb) % LANE_TILE == 0):
            return rb, cb
    raise ValueError(f"no tile-aligned split of a ({r}, {c}) parameter over {steps} grid steps")


CAST_STEPS = 16


def _cast_kernel(*refs):
    n = len(refs) // 2
    for src, dst in zip(refs[:n], refs[n:]):
        dst[...] = src[...].astype(BF16)


def _cast_params(params):
    in_specs, out_specs, shapes = [], [], []
    for w, w_layer in params:
        _, r, c = w.shape
        rb, cb = _cast_blocking(r, c, CAST_STEPS)
        blk = (None, r // rb, c // cb)
        in_specs.append(pl.BlockSpec(
            blk, lambda s, w_layer=w_layer, cb=cb: (w_layer, s // cb, s % cb)))
        out_specs.append(pl.BlockSpec(blk, lambda s, cb=cb: (0, s // cb, s % cb)))
        shapes.append(jax.ShapeDtypeStruct((1, r, c), BF16))
    return pl.pallas_call(
        _cast_kernel,
        grid=(CAST_STEPS,),
        in_specs=in_specs,
        out_specs=out_specs,
        out_shape=shapes,
        compiler_params=_params("arbitrary"),
        name="cast_params",
    )(*[w for w, _ in params])


def _ffn(x, mod, gain, w_in, w_out, *, tm, layer, cond_row, attn_o=None, w_o=None, cast=()):
    bsz, n, _ = x.shape
    nt = n // tm
    steps = bsz * nt
    tok = pl.BlockSpec((1, tm, D_MODEL), lambda b, i: (b, i, 0))
    with_attn_out = attn_o is not None
    attn_args = (attn_o, w_o) if with_attn_out else ()
    attn_specs = [tok, _layer_spec((D_MODEL, D_MODEL), 0)] if with_attn_out else []
    cast_in_specs, cast_out_specs, cast_shapes = [], [], []
    for w, w_layer in cast:
        _, r, c = w.shape
        rb, cb = _cast_blocking(r, c, steps)
        cast_in_specs.append(pl.BlockSpec(
            (None, r // rb, c // cb),
            lambda b, i, w_layer=w_layer, cb=cb: (w_layer, (b * nt + i) // cb, (b * nt + i) % cb)))
        cast_out_specs.append(pl.BlockSpec(
            (None, r // rb, c // cb),
            lambda b, i, cb=cb: (0, (b * nt + i) // cb, (b * nt + i) % cb)))
        cast_shapes.append(jax.ShapeDtypeStruct((1, r, c), BF16))
    outs = pl.pallas_call(
        functools.partial(_ffn_kernel, with_attn_out=with_attn_out, n_cast=len(cast)),
        grid=(bsz, nt),
        in_specs=[tok] + attn_specs + [
            _mod_spec(layer, cond_row),
            _layer_spec((1, D_MODEL), layer),
            _layer_spec((D_MODEL, 2 * FFN_HIDDEN), 0),
            _layer_spec((FFN_HIDDEN, D_MODEL), 0),
        ] + cast_in_specs,
        out_specs=[tok] + cast_out_specs,
        out_shape=[jax.ShapeDtypeStruct(x.shape, F32)] + cast_shapes,
        compiler_params=_params("parallel", "arbitrary"),
        name="attn_out_ffn" if with_attn_out else "ffn",
    )(x, *attn_args, mod, gain, w_in, w_out, *[w for w, _ in cast])
    return outs[0], list(outs[1:])


QKV_CHUNK = 512


def _head_mean_matrix():
    r = lax.broadcasted_iota(jnp.int32, (GROUP_W, GROUP_W), 0) // HEAD_DIM
    c = lax.broadcasted_iota(jnp.int32, (GROUP_W, GROUP_W), 1) // HEAD_DIM
    return jnp.where(r == c, 1.0, 0.0).astype(BF16)


def _store_keys(ref, val, c0, by_col_block):
    tm, w = val.shape
    if not by_col_block:
        ref[0, :, c0:c0 + w] = val
        return
    for row in range(tm // GRID_W):
        for kb in range(N_COL_BLOCKS):
            src = row * GRID_W + kb * COL_BLOCK
            ref[0, kb, row * COL_BLOCK:(row + 1) * COL_BLOCK, c0:c0 + w] = val[src:src + COL_BLOCK]


def _qkv_kernel(x_ref, mod_ref, gain_ref, w_ref, gq_ref, gk_ref,
                q_ref, k_ref, v_ref, *, keys_by_col_block):
    m = mod_ref[...]
    shift = m[0:1]
    gain_scale = gain_ref[...] * (1.0 + m[1:2])
    h = _norm_mod(x_ref[0], gain_scale, shift).astype(BF16)
    ones_bd = _head_mean_matrix()
    gq = gq_ref[...] * (HEAD_DIM ** -0.5)
    gk = gk_ref[...]
    for j in range(D_MODEL // QKV_CHUNK):
        lo = j * QKV_CHUNK
        for base, g, out, permute in ((0, gq, q_ref, False),
                                      (D_MODEL, gk, k_ref, keys_by_col_block)):
            t2 = _dot(h, w_ref[:, base + lo:base + lo + QKV_CHUNK])
            for s in range(QKV_CHUNK // GROUP_W):
                t = t2[:, s * GROUP_W:(s + 1) * GROUP_W]
                ms = _dot((t * t).astype(BF16), ones_bd) * (1.0 / HEAD_DIM)
                normed = (t * lax.rsqrt(ms + EPS) * g).astype(BF16)
                _store_keys(out, normed, lo + s * GROUP_W, permute)
        v = _dot(h, w_ref[:, 2 * D_MODEL + lo:2 * D_MODEL + lo + QKV_CHUNK]).astype(BF16)
        _store_keys(v_ref, v, lo, keys_by_col_block)


def _qkv(x, mod, gain, w_qkv, gq, gk, *, tm, layer, mixer, cond_row, keys_by_col_block):
    bsz, n, _ = x.shape
    tok = pl.BlockSpec((1, tm, D_MODEL), lambda b, i: (b, i, 0))
    out = jax.ShapeDtypeStruct(x.shape, BF16)
    if keys_by_col_block:
        assert tm % GRID_W == 0
        kv_spec = pl.BlockSpec((1, N_COL_BLOCKS, tm // N_COL_BLOCKS, D_MODEL),
                               lambda b, i: (b, 0, i, 0))
        kv_out = jax.ShapeDtypeStruct((bsz, N_COL_BLOCKS, n // N_COL_BLOCKS, D_MODEL), BF16)
    else:
        kv_spec, kv_out = tok, out
    return pl.pallas_call(
        functools.partial(_qkv_kernel, keys_by_col_block=keys_by_col_block),
        grid=(bsz, n // tm),
        in_specs=[
            tok,
            _mod_spec(layer, cond_row),
            _layer_spec((1, D_MODEL), layer),
            _layer_spec((D_MODEL, 3 * D_MODEL), 0),
            _layer_spec((1, GROUP_W), mixer),
            _layer_spec((1, GROUP_W), mixer),
        ],
        out_specs=[tok, kv_spec, kv_spec],
        out_shape=[out, kv_out, kv_out],
        compiler_params=_params("parallel", "arbitrary"),
        name="qkv",
    )(x, mod, gain, w_qkv, gq, gk)


def _stack_heads(q):
    lane_head = lax.broadcasted_iota(jnp.int32, q.shape, 1) // HEAD_DIM
    zero = jnp.zeros_like(q)
    return jnp.concatenate(
        [jnp.where(lane_head == hh, q, zero) for hh in range(HEADS_PER_GROUP)], axis=0)


def _unstack_heads(o, n):
    lane_head = lax.broadcasted_iota(jnp.int32, (n, GROUP_W), 1) // HEAD_DIM
    out = o[0:n]
    for hh in range(1, HEADS_PER_GROUP):
        out = jnp.where(lane_head == hh, o[hh * n:(hh + 1) * n], out)
    return out


def _window_scores(s):
    pieces = []
    for hh in range(HEADS_PER_GROUP):
        for q0, q1, kb0 in QUERY_SEGMENTS:
            pieces.append(s[hh * GRID_W + q0:hh * GRID_W + q1,
                            kb0 * WIN_BLOCK:kb0 * WIN_BLOCK + SEG_KEYS])
    return jnp.concatenate(pieces, axis=0)


def _window_probs(p):
    pieces = []
    for hh in range(HEADS_PER_GROUP):
        for q0, q1, kb0 in QUERY_SEGMENTS:
            n = q1 - q0
            parts = [p[hh * GRID_W + q0:hh * GRID_W + q1]]
            if kb0 > 0:
                parts.insert(0, jnp.zeros((n, kb0 * WIN_BLOCK), p.dtype))
            n_after = N_COL_BLOCKS - 2 - kb0
            if n_after > 0:
                parts.append(jnp.zeros((n, n_after * WIN_BLOCK), p.dtype))
            pieces.append(jnp.concatenate(parts, axis=1))
    return jnp.concatenate(pieces, axis=0)


def _nbr_attn_kernel(q_ref, k_ref, v_ref, kc_ref, vc_ref, bias_ref, o_ref):
    rows = pl.num_programs(1) * ROWS_PER_STEP
    r0 = pl.program_id(1) * ROWS_PER_STEP
    win0 = jnp.clip(r0 - NA_ROWS // 2, 0, rows - KEY_ROWS_PER_STEP)
    n_win = N_COL_BLOCKS * WIN_BLOCK

    for i in range(ROWS_PER_STEP):
        r = r0 + i
        r_start = jnp.clip(r - NA_ROWS // 2, 0, rows - NA_ROWS)
        cls = r - r_start
        koff = pl.multiple_of((r_start - win0) * COL_BLOCK, COL_BLOCK)
        qoff = i * GRID_W
        for g in range(N_GROUPS):
            lo, hi = g * GROUP_W, (g + 1) * GROUP_W
            q = _stack_heads(q_ref[0, pl.ds(qoff, GRID_W), lo:hi])
            k_all = jnp.concatenate(
                [k_ref[0, kb, pl.ds(koff, WIN_BLOCK), lo:hi] for kb in range(N_COL_BLOCKS)]
                + [kc_ref[0, :, lo:hi]], axis=0)
            v_all = jnp.concatenate(
                [v_ref[0, kb, pl.ds(koff, WIN_BLOCK), lo:hi] for kb in range(N_COL_BLOCKS)]
                + [vc_ref[0, :, lo:hi]], axis=0)
            s_all = _dot_nt(q, k_all)
            s_win = _window_scores(s_all[:, :n_win]) + bias_ref[cls, g]
            s = jnp.concatenate([s_win, s_all[:, n_win:]], axis=1)
            p = jnp.exp(s - jnp.max(s, axis=-1, keepdims=True))
            denom = jnp.sum(p, axis=-1, keepdims=True)
            p_all = jnp.concatenate([_window_probs(p[:, :SEG_KEYS]), p[:, SEG_KEYS:]], axis=1)
            pv = _dot(p_all.astype(BF16), v_all)
            o = _unstack_heads(pv / denom, GRID_W)
            o_ref[0, pl.ds(qoff, GRID_W), lo:hi] = o.astype(BF16)


def _nbr_attn(q, k, v, kc, vc, bias):
    bsz, n, _ = q.shape
    rows = n // GRID_W
    tq = ROWS_PER_STEP * GRID_W
    n_ctx = kc.shape[1]
    assert k.shape == (bsz, N_COL_BLOCKS, rows * COL_BLOCK, D_MODEL)

    def key_map(b, i):
        start = jnp.clip(i * ROWS_PER_STEP - NA_ROWS // 2, 0, rows - KEY_ROWS_PER_STEP)
        return (b, 0, start * COL_BLOCK, 0)

    key_spec = pl.BlockSpec(
        (pl.Element(1), pl.Element(N_COL_BLOCKS), pl.Element(KEY_ROWS_PER_STEP * COL_BLOCK),
         pl.Element(D_MODEL)), key_map)
    ctx_spec = pl.BlockSpec((1, n_ctx, D_MODEL), lambda b, i: (b, 0, 0))
    return pl.pallas_call(
        _nbr_attn_kernel,
        grid=(bsz, rows // ROWS_PER_STEP),
        in_specs=[
            pl.BlockSpec((1, tq, D_MODEL), lambda b, i: (b, i, 0)),
            key_spec, key_spec, ctx_spec, ctx_spec,
            _const_spec(bias.shape),
        ],
        out_specs=pl.BlockSpec((1, tq, D_MODEL), lambda b, i: (b, i, 0)),
        out_shape=jax.ShapeDtypeStruct(q.shape, BF16),
        compiler_params=_params("parallel", "arbitrary"),
        name="nbr_attn",
    )(q, k, v, kc, vc, bias)


def _ctx_attn_kernel(q_ref, k_ref, v_ref, o_ref):
    n = q_ref.shape[1]
    for g in range(N_GROUPS):
        lo, hi = g * GROUP_W, (g + 1) * GROUP_W
        q = _stack_heads(q_ref[0, :, lo:hi])
        s = _dot_nt(q, k_ref[0, :, lo:hi])
        p = jnp.exp(s - jnp.max(s, axis=-1, keepdims=True))
        denom = jnp.sum(p, axis=-1, keepdims=True)
        pv = _dot(p.astype(BF16), v_ref[0, :, lo:hi])
        o_ref[0, :, lo:hi] = _unstack_heads(pv / denom, n).astype(BF16)


def _ctx_attn(q, k, v):
    bsz, n, _ = q.shape
    spec = pl.BlockSpec((1, n, D_MODEL), lambda b: (b, 0, 0))
    return pl.pallas_call(
        _ctx_attn_kernel,
        grid=(bsz,),
        in_specs=[spec, spec, spec],
        out_specs=spec,
        out_shape=jax.ShapeDtypeStruct(q.shape, BF16),
        compiler_params=_params("parallel"),
        name="ctx_attn",
    )(q, k, v)


def _bias_table(rpb):
    q = jnp.arange(GRID_W)[:, None]
    lane = jnp.arange(SEG_KEYS)[None, :]
    kb0 = jnp.concatenate([jnp.full((q1 - q0,), s_kb0) for q0, q1, s_kb0 in QUERY_SEGMENTS])
    k = (kb0[:, None] + lane // WIN_BLOCK) * COL_BLOCK + lane % COL_BLOCK
    j_of_lane = (lane[0] % WIN_BLOCK) // COL_BLOCK
    k_start = jnp.clip(q - NA_COLS // 2, 0, GRID_W - NA_COLS)
    col_ok = (k >= k_start) & (k < k_start + NA_COLS)
    dc = jnp.clip(k - q + NA_COLS - 1, 0, 2 * NA_COLS - 2)
    onehot = (dc[None] == jnp.arange(2 * NA_COLS - 1)[:, None, None]).astype(F32)
    by_dr = jnp.einsum("hrc,cql->hrql", rpb.astype(F32), onehot,
                       precision=lax.Precision.HIGHEST)
    row_sel = (j_of_lane[None] == jnp.arange(NA_ROWS)[:, None]).astype(F32)
    t = jnp.stack([
        sum(by_dr[:, j - cls + NA_ROWS - 1] * row_sel[j] for j in range(NA_ROWS))
        for cls in range(NA_ROWS)])
    t = jnp.where(col_ok[None, None], t, NEG_INF)
    return t.reshape(NA_ROWS, N_GROUPS, HEADS_PER_GROUP * GRID_W, SEG_KEYS)


LATENT_TILE = 1024
CTX_TILE = 512
COND_ROWS = 16


def kernel(x, c, ctx, c_ctx, w_ada, b_ada, norm_mix, norm_ffn, conv_w_in, conv_w,
           conv_w_out, attn_w_qkv, attn_q_norm, attn_k_norm, attn_rpb, attn_w_out,
           ffn_w_in, ffn_w_out):
    bsz, n, d = x.shape
    n_ctx = ctx.shape[1]
    assert d == D_MODEL and n % (ROWS_PER_STEP * GRID_W) == 0 and n % LATENT_TILE == 0
    assert n // GRID_W >= KEY_ROWS_PER_STEP and n_ctx % HALO == 0
    assert (bsz * n_ctx) % CTX_TILE == 0 and bsz < COND_ROWS
    flat = (1, bsz * n_ctx, d)
    per_batch = (bsz, n_ctx, d)
    ctx_row = bsz

    cond = jnp.concatenate(
        [c, c_ctx[None], jnp.zeros((COND_ROWS - bsz - 1, d), F32)], axis=0)
    mod = _ada(cond, w_ada, b_ada).reshape(DEPTH, COND_ROWS, 6, d)

    g_mix = norm_mix.reshape(DEPTH, 1, d)
    g_ffn = norm_ffn.reshape(DEPTH, 1, d)
    gq = jnp.tile(attn_q_norm, (1, HEADS_PER_GROUP))[:, None]
    gk = jnp.tile(attn_k_norm, (1, HEADS_PER_GROUP))[:, None]

    def matmul_params(i):
        mix_in, mix_out = (conv_w_in, conv_w_out) if i % 2 == 0 else (attn_w_qkv, attn_w_out)
        return [(mix_in, i // 2), (mix_out, i // 2), (ffn_w_in, i), (ffn_w_out, i)]

    m_in, m_out, f_in, f_out = _cast_params(matmul_params(0))

    for i in range(DEPTH):
        update_ctx = i < DEPTH - 1
        j = i // 2
        lat = dict(layer=i, cond_row=None)
        con = dict(layer=i, cond_row=ctx_row)
        cast = matmul_params(i + 1) if i + 1 < DEPTH else ()
        if i % 2 == 0:
            x = _conv_mixer(x, mod, g_mix, m_in, conv_w, m_out, tm=LATENT_TILE, mixer=j, **lat)
            x, nxt = _ffn(x, mod, g_ffn, f_in, f_out, tm=LATENT_TILE, cast=cast, **lat)
            if update_ctx:
                ctx = _conv_mixer(ctx, mod, g_mix, m_in, conv_w, m_out, tm=n_ctx, mixer=j, **con)
                ctx = _ffn(ctx.reshape(flat), mod, g_ffn, f_in, f_out, tm=CTX_TILE,
                           **con)[0].reshape(per_batch)
        else:
            q, k, v = _qkv(x, mod, g_mix, m_in, gq, gk, tm=LATENT_TILE, mixer=j,
                           keys_by_col_block=True, **lat)
            qc, kc, vc = (t.reshape(per_batch) for t in _qkv(
                ctx.reshape(flat), mod, g_mix, m_in, gq, gk, tm=CTX_TILE, mixer=j,
                keys_by_col_block=False, **con))
            o = _nbr_attn(q, k, v, kc, vc, _bias_table(attn_rpb[j]))
            x, nxt = _ffn(x, mod, g_ffn, f_in, f_out, tm=LATENT_TILE, attn_o=o, w_o=m_out,
                          cast=cast, **lat)
            if update_ctx:
                oc = _ctx_attn(qc, kc, vc)
                ctx = _ffn(ctx.reshape(flat), mod, g_ffn, f_in, f_out, tm=CTX_TILE,
                           attn_o=oc.reshape(flat), w_o=m_out, **con)[0].reshape(per_batch)
        if nxt:
            m_in, m_out, f_in, f_out = nxt
    return x
```

```python
import functools

import jax
import jax.numpy as jnp
from jax import lax
from jax.experimental import pallas as pl
from jax.experimental.pallas import tpu as pltpu

D_MODEL = 1024
DEPTH = 4
GRID_W = 64
N_HEADS = 16
HEAD_DIM = D_MODEL // N_HEADS
NA_ROWS = 8
NA_COLS = 16
FFN_HIDDEN = 2816
EPS = 1e-6
NEG_INF = -1e30
LOG2E = 1.4426950408889634

F32 = jnp.float32
BF16 = jnp.bfloat16

VMEM_LIMIT_BYTES = 56 * 1024 * 1024
HEADS_PER_GROUP = 4
GROUP_W = HEADS_PER_GROUP * HEAD_DIM
N_GROUPS = N_HEADS // HEADS_PER_GROUP
ROWS_PER_STEP = 8
KEY_ROWS_PER_STEP = ROWS_PER_STEP + NA_ROWS
COL_BLOCK = NA_COLS
N_COL_BLOCKS = GRID_W // COL_BLOCK
WIN_BLOCK = NA_ROWS * COL_BLOCK
QUERY_TILE = 8


def _query_segments():
    segs = []
    for q0 in range(0, GRID_W, QUERY_TILE):
        first = min(max(q0 - NA_COLS // 2, 0), GRID_W - NA_COLS)
        last = min(max(q0 + QUERY_TILE - 1 - NA_COLS // 2, 0), GRID_W - NA_COLS) + NA_COLS
        kb0 = min(first // COL_BLOCK, N_COL_BLOCKS - 2)
        assert last <= (kb0 + 2) * COL_BLOCK
        if segs and segs[-1][2] == kb0:
            segs[-1] = (segs[-1][0], q0 + QUERY_TILE, kb0)
        else:
            segs.append((q0, q0 + QUERY_TILE, kb0))
    return tuple(segs)


QUERY_SEGMENTS = _query_segments()
SEG_KEYS = 2 * WIN_BLOCK


def _params(*sem):
    return pltpu.CompilerParams(dimension_semantics=sem,
                                vmem_limit_bytes=VMEM_LIMIT_BYTES)


def _const_spec(shape):
    nd = len(shape)
    return pl.BlockSpec(shape, lambda *_: (0,) * nd, pipeline_mode=pl.Buffered(1))


def _layer_spec(shape, layer):
    nd = len(shape)
    return pl.BlockSpec((None,) + tuple(shape), lambda *_: (layer,) + (0,) * nd,
                        pipeline_mode=pl.Buffered(1))


def _mod_spec(layer, cond_row):
    if cond_row is None:
        return pl.BlockSpec((None, None, 6, D_MODEL), lambda b, i: (layer, b, 0, 0))
    return pl.BlockSpec((None, None, 6, D_MODEL), lambda b, i: (layer, cond_row, 0, 0))


def _silu(x):
    return x / (1.0 + jnp.exp(-x))


def _norm_mod(x, gain_scale, shift):
    ms = jnp.mean(x * x, axis=-1, keepdims=True)
    return x * lax.rsqrt(ms + EPS) * gain_scale + shift


def _dot(a, b):
    return jnp.dot(a, b, preferred_element_type=F32)


def _dot_nt(a, b):
    return lax.dot_general(a, b, (((1,), (1,)), ((), ())),
                           preferred_element_type=F32)


ADA_COLS = 2048


def _ada_kernel(c_ref, w_ref, b_ref, o_ref):
    a = _silu(c_ref[...]).astype(BF16)
    o_ref[0] = _dot(a, w_ref[0].astype(BF16)) + b_ref[0]


def _ada(cond, w_ada, b_ada):
    n_rows = cond.shape[0]
    n_cols = w_ada.shape[-1]
    return pl.pallas_call(
        _ada_kernel,
        grid=(DEPTH, n_cols // ADA_COLS),
        in_specs=[
            pl.BlockSpec((n_rows, D_MODEL), lambda l, j: (0, 0)),
            pl.BlockSpec((1, D_MODEL, ADA_COLS), lambda l, j: (l, 0, j)),
            pl.BlockSpec((1, 1, ADA_COLS), lambda l, j: (l, 0, j)),
        ],
        out_specs=pl.BlockSpec((1, n_rows, ADA_COLS), lambda l, j: (l, 0, j)),
        out_shape=jax.ShapeDtypeStruct((DEPTH, n_rows, n_cols), F32),
        compiler_params=_params("arbitrary", "arbitrary"),
        name="ada",
    )(cond, w_ada, b_ada.reshape(DEPTH, 1, n_cols))


CONV_CHUNK = 256
HALO = 8


def _conv_kernel(x_ref, xp_ref, xn_ref, mod_ref, gain_ref, win_ref, cw_ref,
                 wout_ref, o_ref, *, tm):
    i = pl.program_id(1)
    last = pl.num_programs(1) - 1
    m = mod_ref[...]
    shift = m[0:1]
    gain_scale = gain_ref[...] * (1.0 + m[1:2])
    gate = m[2:3]

    x = x_ref[0]
    x_ext = jnp.concatenate([x, xp_ref[0], xn_ref[0]], axis=0)
    h_ext = _norm_mod(x_ext, gain_scale, shift).astype(BF16)
    h = h_ext[:tm]
    has_prev = i > 0
    has_next = i < last

    cw = cw_ref[...]
    rows = lax.broadcasted_iota(jnp.int32, (tm, CONV_CHUNK), 0)
    gated = []
    for j in range(D_MODEL // CONV_CHUNK):
        lo, hi = j * CONV_CHUNK, (j + 1) * CONV_CHUNK
        b_gate = _dot(h, win_ref[:, lo:hi])
        c_gate = _dot(h_ext, win_ref[:, D_MODEL + lo:D_MODEL + hi])
        u = _dot(h_ext, win_ref[:, 2 * D_MODEL + lo:2 * D_MODEL + hi])
        z_ext = c_gate * u
        z = z_ext[:tm]
        z_prev = jnp.where(has_prev, z_ext[tm + HALO - 1:tm + HALO], 0.0)
        z_next = jnp.where(has_next, z_ext[tm + HALO:tm + HALO + 1], 0.0)
        z_up = jnp.where(rows == 0, z_prev, pltpu.roll(z, 1, 0))
        z_dn = jnp.where(rows == tm - 1, z_next, pltpu.roll(z, tm - 1, 0))
        zc = cw[0:1, lo:hi] * z_up + cw[1:2, lo:hi] * z + cw[2:3, lo:hi] * z_dn
        gated.append((b_gate * zc).astype(BF16))
    o_ref[0] = x + gate * _dot(jnp.concatenate(gated, axis=1), wout_ref[...])


def _conv_mixer(x, mod, gain, w_in, conv_w, w_out, *, tm, layer, mixer, cond_row):
    bsz, n, _ = x.shape
    nt = n // tm
    hb = tm // HALO
    n_hb = n // HALO
    return pl.pallas_call(
        functools.partial(_conv_kernel, tm=tm),
        grid=(bsz, nt),
        in_specs=[
            pl.BlockSpec((1, tm, D_MODEL), lambda b, i: (b, i, 0)),
            pl.BlockSpec((1, HALO, D_MODEL),
                         lambda b, i: (b, jnp.maximum(i * hb - 1, 0), 0)),
            pl.BlockSpec((1, HALO, D_MODEL),
                         lambda b, i: (b, jnp.minimum((i + 1) * hb, n_hb - 1), 0)),
            _mod_spec(layer, cond_row),
            _layer_spec((1, D_MODEL), layer),
            _layer_spec((D_MODEL, 3 * D_MODEL), 0),
            _layer_spec((3, D_MODEL), mixer),
            _layer_spec((D_MODEL, D_MODEL), 0),
        ],
        out_specs=pl.BlockSpec((1, tm, D_MODEL), lambda b, i: (b, i, 0)),
        out_shape=jax.ShapeDtypeStruct(x.shape, F32),
        compiler_params=_params("parallel", "arbitrary"),
        name="conv_mixer",
    )(x, x, x, mod, gain, w_in, conv_w, w_out)


FFN_CHUNK = 256
FFN_ROW_TILE = 256


def _ffn_kernel(*refs, with_attn_out, n_cast):
    n_in = len(refs) - 1 - 2 * n_cast
    cast_in = refs[n_in:n_in + n_cast]
    o_ref = refs[n_in + n_cast]
    cast_out = refs[n_in + n_cast + 1:]
    if with_attn_out:
        x_ref, ao_ref, wo_ref, mod_ref, gain_ref, win_ref, wout_ref = refs[:n_in]
    else:
        x_ref, mod_ref, gain_ref, win_ref, wout_ref = refs[:n_in]
    for src, dst in zip(cast_in, cast_out):
        dst[...] = src[...].astype(BF16)
    m = mod_ref[...]
    shift = m[3:4]
    gain_scale = gain_ref[...] * (1.0 + m[4:5])
    gate = m[5:6]
    x = x_ref[0]
    if with_attn_out:
        x = x + m[2:3] * _dot(ao_ref[0], wo_ref[...])
    h = _norm_mod(x, gain_scale, shift).astype(BF16)
    for r in range(0, x.shape[0], FFN_ROW_TILE):
        h_r = h[r:r + FFN_ROW_TILE]
        acts = []
        for lo in range(0, FFN_HIDDEN, FFN_CHUNK):
            hi = min(lo + FFN_CHUNK, FFN_HIDDEN)
            g = _dot(h_r, win_ref[:, lo:hi])
            u = _dot(h_r, win_ref[:, FFN_HIDDEN + lo:FFN_HIDDEN + hi])
            acts.append((_silu(g) * u).astype(BF16))
        y = _dot(jnp.concatenate(acts, axis=1), wout_ref[...])
        o_ref[0, r:r + FFN_ROW_TILE] = x[r:r + FFN_ROW_TILE] + gate * y


BF16_ROW_TILE = 16
LANE_TILE = 128


def _cast_blocking(r, c, steps):
    for cb in range(1, steps + 1):
        rb = steps // cb
        if (steps % cb == 0 and r % rb == 0 and c % cb == 0
                and (r // rb) % BF16_ROW_TILE == 0 and (c // cb) % LANE_TILE == 0):
            return rb, cb
    raise ValueError(f"no tile-aligned split of a ({r}, {c}) parameter over {steps} grid steps")


CAST_STEPS = 16


def _cast_kernel(*refs):
    n = len(refs) // 2
    for src, dst in zip(refs[:n], refs[n:]):
        dst[...] = src[...].astype(BF16)


def _cast_params(params):
    in_specs, out_specs, shapes = [], [], []
    for w, w_layer in params:
        _, r, c = w.shape
        rb, cb = _cast_blocking(r, c, CAST_STEPS)
        blk = (None, r // rb, c // cb)
        in_specs.append(pl.BlockSpec(
            blk, lambda s, w_layer=w_layer, cb=cb: (w_layer, s // cb, s % cb)))
        out_specs.append(pl.BlockSpec(blk, lambda s, cb=cb: (0, s // cb, s % cb)))
        shapes.append(jax.ShapeDtypeStruct((1, r, c), BF16))
    return pl.pallas_call(
        _cast_kernel,
        grid=(CAST_STEPS,),
        in_specs=in_specs,
        out_specs=out_specs,
        out_shape=shapes,
        compiler_params=_params("arbitrary"),
        name="cast_params",
    )(*[w for w, _ in params])


def _ffn(x, mod, gain, w_in, w_out, *, tm, layer, cond_row, attn_o=None, w_o=None, cast=()):
    bsz, n, _ = x.shape
    nt = n // tm
    steps = bsz * nt
    tok = pl.BlockSpec((1, tm, D_MODEL), lambda b, i: (b, i, 0))
    with_attn_out = attn_o is not None
    attn_args = (attn_o, w_o) if with_attn_out else ()
    attn_specs = [tok, _layer_spec((D_MODEL, D_MODEL), 0)] if with_attn_out else []
    cast_in_specs, cast_out_specs, cast_shapes = [], [], []
    for w, w_layer in cast:
        _, r, c = w.shape
        rb, cb = _cast_blocking(r, c, steps)
        cast_in_specs.append(pl.BlockSpec(
            (None, r // rb, c // cb),
            lambda b, i, w_layer=w_layer, cb=cb: (w_layer, (b * nt + i) // cb, (b * nt + i) % cb)))
        cast_out_specs.append(pl.BlockSpec(
            (None, r // rb, c // cb),
            lambda b, i, cb=cb: (0, (b * nt + i) // cb, (b * nt + i) % cb)))
        cast_shapes.append(jax.ShapeDtypeStruct((1, r, c), BF16))
    outs = pl.pallas_call(
        functools.partial(_ffn_kernel, with_attn_out=with_attn_out, n_cast=len(cast)),
        grid=(bsz, nt),
        in_specs=[tok] + attn_specs + [
            _mod_spec(layer, cond_row),
            _layer_spec((1, D_MODEL), layer),
            _layer_spec((D_MODEL, 2 * FFN_HIDDEN), 0),
            _layer_spec((FFN_HIDDEN, D_MODEL), 0),
        ] + cast_in_specs,
        out_specs=[tok] + cast_out_specs,
        out_shape=[jax.ShapeDtypeStruct(x.shape, F32)] + cast_shapes,
        compiler_params=_params("parallel", "arbitrary"),
        name="attn_out_ffn" if with_attn_out else "ffn",
    )(x, *attn_args, mod, gain, w_in, w_out, *[w for w, _ in cast])
    return outs[0], list(outs[1:])


QKV_CHUNK = 512


def _head_mean_matrix():
    r = lax.broadcasted_iota(jnp.int32, (GROUP_W, GROUP_W), 0) // HEAD_DIM
    c = lax.broadcasted_iota(jnp.int32, (GROUP_W, GROUP_W), 1) // HEAD_DIM
    return jnp.where(r == c, 1.0, 0.0).astype(BF16)


def _store_keys(ref, val, c0, by_col_block):
    tm, w = val.shape
    if not by_col_block:
        ref[0, :, c0:c0 + w] = val
        return
    for row in range(tm // GRID_W):
        for kb in range(N_COL_BLOCKS):
            src = row * GRID_W + kb * COL_BLOCK
            ref[0, kb, row * COL_BLOCK:(row + 1) * COL_BLOCK, c0:c0 + w] = val[src:src + COL_BLOCK]


def _qkv_kernel(x_ref, mod_ref, gain_ref, w_ref, gq_ref, gk_ref,
                q_ref, k_ref, v_ref, *, keys_by_col_block):
    m = mod_ref[...]
    shift = m[0:1]
    gain_scale = gain_ref[...] * (1.0 + m[1:2])
    h = _norm_mod(x_ref[0], gain_scale, shift).astype(BF16)
    ones_bd = _head_mean_matrix()
    gq = gq_ref[...] * (HEAD_DIM ** -0.5 * LOG2E)
    gk = gk_ref[...]
    for j in range(D_MODEL // QKV_CHUNK):
        lo = j * QKV_CHUNK
        for base, g, out, permute in ((0, gq, q_ref, False),
                                      (D_MODEL, gk, k_ref, keys_by_col_block)):
            t2 = _dot(h, w_ref[:, base + lo:base + lo + QKV_CHUNK])
            for s in range(QKV_CHUNK // GROUP_W):
                t = t2[:, s * GROUP_W:(s + 1) * GROUP_W]
                ms = _dot((t * t).astype(BF16), ones_bd) * (1.0 / HEAD_DIM)
                normed = (t * lax.rsqrt(ms + EPS) * g).astype(BF16)
                _store_keys(out, normed, lo + s * GROUP_W, permute)
        v = _dot(h, w_ref[:, 2 * D_MODEL + lo:2 * D_MODEL + lo + QKV_CHUNK]).astype(BF16)
        _store_keys(v_ref, v, lo, keys_by_col_block)


def _qkv(x, mod, gain, w_qkv, gq, gk, *, tm, layer, mixer, cond_row, keys_by_col_block):
    bsz, n, _ = x.shape
    tok = pl.BlockSpec((1, tm, D_MODEL), lambda b, i: (b, i, 0))
    out = jax.ShapeDtypeStruct(x.shape, BF16)
    if keys_by_col_block:
        assert tm % GRID_W == 0
        kv_spec = pl.BlockSpec((1, N_COL_BLOCKS, tm // N_COL_BLOCKS, D_MODEL),
                               lambda b, i: (b, 0, i, 0))
        kv_out = jax.ShapeDtypeStruct((bsz, N_COL_BLOCKS, n // N_COL_BLOCKS, D_MODEL), BF16)
    else:
        kv_spec, kv_out = tok, out
    return pl.pallas_call(
        functools.partial(_qkv_kernel, keys_by_col_block=keys_by_col_block),
        grid=(bsz, n // tm),
        in_specs=[
            tok,
            _mod_spec(layer, cond_row),
            _layer_spec((1, D_MODEL), layer),
            _layer_spec((D_MODEL, 3 * D_MODEL), 0),
            _layer_spec((1, GROUP_W), mixer),
            _layer_spec((1, GROUP_W), mixer),
        ],
        out_specs=[tok, kv_spec, kv_spec],
        out_shape=[out, kv_out, kv_out],
        compiler_params=_params("parallel", "arbitrary"),
        name="qkv",
    )(x, mod, gain, w_qkv, gq, gk)


def _stack_heads(q):
    lane_head = lax.broadcasted_iota(jnp.int32, q.shape, 1) // HEAD_DIM
    zero = jnp.zeros_like(q)
    return jnp.concatenate(
        [jnp.where(lane_head == hh, q, zero) for hh in range(HEADS_PER_GROUP)], axis=0)


def _unstack_heads(o, n):
    lane_head = lax.broadcasted_iota(jnp.int32, (n, GROUP_W), 1) // HEAD_DIM
    out = o[0:n]
    for hh in range(1, HEADS_PER_GROUP):
        out = jnp.where(lane_head == hh, o[hh * n:(hh + 1) * n], out)
    return out


def _window_scores(s):
    pieces = []
    for hh in range(HEADS_PER_GROUP):
        for q0, q1, kb0 in QUERY_SEGMENTS:
            pieces.append(s[hh * GRID_W + q0:hh * GRID_W + q1,
                            kb0 * WIN_BLOCK:kb0 * WIN_BLOCK + SEG_KEYS])
    return jnp.concatenate(pieces, axis=0)


def _window_probs(p):
    pieces = []
    for hh in range(HEADS_PER_GROUP):
        for q0, q1, kb0 in QUERY_SEGMENTS:
            n = q1 - q0
            parts = [p[hh * GRID_W + q0:hh * GRID_W + q1]]
            if kb0 > 0:
                parts.insert(0, jnp.zeros((n, kb0 * WIN_BLOCK), p.dtype))
            n_after = N_COL_BLOCKS - 2 - kb0
            if n_after > 0:
                parts.append(jnp.zeros((n, n_after * WIN_BLOCK), p.dtype))
            pieces.append(jnp.concatenate(parts, axis=1))
    return jnp.concatenate(pieces, axis=0)


def _nbr_attn_kernel(q_ref, k_ref, v_ref, kc_ref, vc_ref, bias_ref, o_ref):
    rows = pl.num_programs(1) * ROWS_PER_STEP
    r0 = pl.program_id(1) * ROWS_PER_STEP
    win0 = jnp.clip(r0 - NA_ROWS // 2, 0, rows - KEY_ROWS_PER_STEP)
    n_win = N_COL_BLOCKS * WIN_BLOCK

    for i in range(ROWS_PER_STEP):
        r = r0 + i
        r_start = jnp.clip(r - NA_ROWS // 2, 0, rows - NA_ROWS)
        cls = r - r_start
        koff = pl.multiple_of((r_start - win0) * COL_BLOCK, COL_BLOCK)
        qoff = i * GRID_W
        for g in range(N_GROUPS):
            lo, hi = g * GROUP_W, (g + 1) * GROUP_W
            q = _stack_heads(q_ref[0, pl.ds(qoff, GRID_W), lo:hi])
            k_all = jnp.concatenate(
                [k_ref[0, kb, pl.ds(koff, WIN_BLOCK), lo:hi] for kb in range(N_COL_BLOCKS)]
                + [kc_ref[0, :, lo:hi]], axis=0)
            v_all = jnp.concatenate(
                [v_ref[0, kb, pl.ds(koff, WIN_BLOCK), lo:hi] for kb in range(N_COL_BLOCKS)]
                + [vc_ref[0, :, lo:hi]], axis=0)
            s_all = _dot_nt(q, k_all)
            s_win = _window_scores(s_all[:, :n_win]) + bias_ref[cls, g]
            s = jnp.concatenate([s_win, s_all[:, n_win:]], axis=1)
            p = jnp.exp2(s - jnp.max(s, axis=-1, keepdims=True))
            denom = jnp.sum(p, axis=-1, keepdims=True)
            p_all = jnp.concatenate([_window_probs(p[:, :SEG_KEYS]), p[:, SEG_KEYS:]], axis=1)
            pv = _dot(p_all.astype(BF16), v_all)
            o = _unstack_heads(pv / denom, GRID_W)
            o_ref[0, pl.ds(qoff, GRID_W), lo:hi] = o.astype(BF16)


def _nbr_attn(q, k, v, kc, vc, bias):
    bsz, n, _ = q.shape
    rows = n // GRID_W
    tq = ROWS_PER_STEP * GRID_W
    n_ctx = kc.shape[1]
    assert k.shape == (bsz, N_COL_BLOCKS, rows * COL_BLOCK, D_MODEL)

    def key_map(b, i):
        start = jnp.clip(i * ROWS_PER_STEP - NA_ROWS // 2, 0, rows - KEY_ROWS_PER_STEP)
        return (b, 0, start * COL_BLOCK, 0)

    key_spec = pl.BlockSpec(
        (pl.Element(1), pl.Element(N_COL_BLOCKS), pl.Element(KEY_ROWS_PER_STEP * COL_BLOCK),
         pl.Element(D_MODEL)), key_map)
    ctx_spec = pl.BlockSpec((1, n_ctx, D_MODEL), lambda b, i: (b, 0, 0))
    return pl.pallas_call(
        _nbr_attn_kernel,
        grid=(bsz, rows // ROWS_PER_STEP),
        in_specs=[
            pl.BlockSpec((1, tq, D_MODEL), lambda b, i: (b, i, 0)),
            key_spec, key_spec, ctx_spec, ctx_spec,
            _const_spec(bias.shape),
        ],
        out_specs=pl.BlockSpec((1, tq, D_MODEL), lambda b, i: (b, i, 0)),
        out_shape=jax.ShapeDtypeStruct(q.shape, BF16),
        compiler_params=_params("parallel", "arbitrary"),
        name="nbr_attn",
    )(q, k, v, kc, vc, bias)


def _ctx_attn_kernel(q_ref, k_ref, v_ref, o_ref):
    n = q_ref.shape[1]
    for g in range(N_GROUPS):
        lo, hi = g * GROUP_W, (g + 1) * GROUP_W
        q = _stack_heads(q_ref[0, :, lo:hi])
        s = _dot_nt(q, k_ref[0, :, lo:hi])
        p = jnp.exp2(s - jnp.max(s, axis=-1, keepdims=True))
        denom = jnp.sum(p, axis=-1, keepdims=True)
        pv = _dot(p.astype(BF16), v_ref[0, :, lo:hi])
        o_ref[0, :, lo:hi] = _unstack_heads(pv / denom, n).astype(BF16)


def _ctx_attn(q, k, v):
    bsz, n, _ = q.shape
    spec = pl.BlockSpec((1, n, D_MODEL), lambda b: (b, 0, 0))
    return pl.pallas_call(
        _ctx_attn_kernel,
        grid=(bsz,),
        in_specs=[spec, spec, spec],
        out_specs=spec,
        out_shape=jax.ShapeDtypeStruct(q.shape, BF16),
        compiler_params=_params("parallel"),
        name="ctx_attn",
    )(q, k, v)


def _bias_table(rpb):
    q = jnp.arange(GRID_W)[:, None]
    lane = jnp.arange(SEG_KEYS)[None, :]
    kb0 = jnp.concatenate([jnp.full((q1 - q0,), s_kb0) for q0, q1, s_kb0 in QUERY_SEGMENTS])
    k = (kb0[:, None] + lane // WIN_BLOCK) * COL_BLOCK + lane % COL_BLOCK
    j_of_lane = (lane[0] % WIN_BLOCK) // COL_BLOCK
    k_start = jnp.clip(q - NA_COLS // 2, 0, GRID_W - NA_COLS)
    col_ok = (k >= k_start) & (k < k_start + NA_COLS)
    dc = jnp.clip(k - q + NA_COLS - 1, 0, 2 * NA_COLS - 2)
    onehot = (dc[None] == jnp.arange(2 * NA_COLS - 1)[:, None, None]).astype(F32)
    by_dr = jnp.einsum("hrc,cql->hrql", rpb.astype(F32), onehot,
                       precision=lax.Precision.HIGHEST)
    row_sel = (j_of_lane[None] == jnp.arange(NA_ROWS)[:, None]).astype(F32)
    t = jnp.stack([
        sum(by_dr[:, j - cls + NA_ROWS - 1] * row_sel[j] for j in range(NA_ROWS))
        for cls in range(NA_ROWS)])
    t = jnp.where(col_ok[None, None], t * LOG2E, NEG_INF)
    return t.reshape(NA_ROWS, N_GROUPS, HEADS_PER_GROUP * GRID_W, SEG_KEYS)


LATENT_TILE = 1024
CTX_TILE = 512
COND_ROWS = 16


def kernel(x, c, ctx, c_ctx, w_ada, b_ada, norm_mix, norm_ffn, conv_w_in, conv_w,
           conv_w_out, attn_w_qkv, attn_q_norm, attn_k_norm, attn_rpb, attn_w_out,
           ffn_w_in, ffn_w_out):
    bsz, n, d = x.shape
    n_ctx = ctx.shape[1]
    assert d == D_MODEL and n % (ROWS_PER_STEP * GRID_W) == 0 and n % LATENT_TILE == 0
    assert n // GRID_W >= KEY_ROWS_PER_STEP and n_ctx % HALO == 0
    assert (bsz * n_ctx) % CTX_TILE == 0 and bsz < COND_ROWS
    flat = (1, bsz * n_ctx, d)
    per_batch = (bsz, n_ctx, d)
    ctx_row = bsz

    cond = jnp.concatenate(
        [c, c_ctx[None], jnp.zeros((COND_ROWS - bsz - 1, d), F32)], axis=0)
    mod = _ada(cond, w_ada, b_ada).reshape(DEPTH, COND_ROWS, 6, d)

    g_mix = norm_mix.reshape(DEPTH, 1, d)
    g_ffn = norm_ffn.reshape(DEPTH, 1, d)
    gq = jnp.tile(attn_q_norm, (1, HEADS_PER_GROUP))[:, None]
    gk = jnp.tile(attn_k_norm, (1, HEADS_PER_GROUP))[:, None]

    def matmul_params(i):
        mix_in, mix_out = (conv_w_in, conv_w_out) if i % 2 == 0 else (attn_w_qkv, attn_w_out)
        return [(mix_in, i // 2), (mix_out, i // 2), (ffn_w_in, i), (ffn_w_out, i)]

    m_in, m_out, f_in, f_out = _cast_params(matmul_params(0))

    for i in range(DEPTH):
        update_ctx = i < DEPTH - 1
        j = i // 2
        lat = dict(layer=i, cond_row=None)
        con = dict(layer=i, cond_row=ctx_row)
        cast = matmul_params(i + 1) if i + 1 < DEPTH else ()
        if i % 2 == 0:
            x = _conv_mixer(x, mod, g_mix, m_in, conv_w, m_out, tm=LATENT_TILE, mixer=j, **lat)
            x, nxt = _ffn(x, mod, g_ffn, f_in, f_out, tm=LATENT_TILE, cast=cast, **lat)
            if update_ctx:
                ctx = _conv_mixer(ctx, mod, g_mix, m_in, conv_w, m_out, tm=n_ctx, mixer=j, **con)
                ctx = _ffn(ctx.reshape(flat), mod, g_ffn, f_in, f_out, tm=CTX_TILE,
                           **con)[0].reshape(per_batch)
        else:
            q, k, v = _qkv(x, mod, g_mix, m_in, gq, gk, tm=LATENT_TILE, mixer=j,
                           keys_by_col_block=True, **lat)
            qc, kc, vc = (t.reshape(per_batch) for t in _qkv(
                ctx.reshape(flat), mod, g_mix, m_in, gq, gk, tm=CTX_TILE, mixer=j,
                keys_by_col_block=False, **con))
            o = _nbr_attn(q, k, v, kc, vc, _bias_table(attn_rpb[j]))
            x, nxt = _ffn(x, mod, g_ffn, f_in, f_out, tm=LATENT_TILE, attn_o=o, w_o=m_out,
                          cast=cast, **lat)
            if update_ctx:
                oc = _ctx_attn(qc, kc, vc)
                ctx = _ffn(ctx.reshape(flat), mod, g_ffn, f_in, f_out, tm=CTX_TILE,
                           attn_o=oc.reshape(flat), w_o=m_out, **con)[0].reshape(per_batch)
        if nxt:
            m_in, m_out, f_in, f_out = nxt
    return x
```
